```python
import math
import jax, jax.numpy as jnp
from jax import lax
import numpy as np

D_MODEL = 2048
BATCH = 2
SEQ = 8192
DEPTH = 1
DEC_BATCH = 2
DEC_SEQ = 4096
PAST_LEN = 128

N_Q_HEADS = 16
N_KV_HEADS = 4
HEAD_DIM = 64
Q_PER_KV = N_Q_HEADS // N_KV_HEADS
WINDOW = 128
BLOCK = 128
ATTN_W = N_Q_HEADS * HEAD_DIM
KV_W = N_KV_HEADS * HEAD_DIM
NEG_BIG = -1e30
D_INNER = D_MODEL
SSM_HEAD_DIM = 64
N_SSM_HEADS = D_INNER // SSM_HEAD_DIM
N_SSM_GROUPS = 4
HEADS_PER_GROUP = N_SSM_HEADS // N_SSM_GROUPS
D_STATE = 128
D_CONV = 5
CHUNK = 128
XBC_W = D_INNER + 2 * N_SSM_GROUPS * D_STATE
N_BRANCH = 2
Q_END = ATTN_W
K_END = Q_END + KV_W
V_END = K_END + KV_W
Z_END = V_END + D_INNER
XBC_END = Z_END + XBC_W
DT_START = XBC_END
DT_END = DT_START + 2 * N_SSM_HEADS
IN_W = DT_END + N_BRANCH * D_MODEL
SPLIT_POINTS = [Q_END, K_END, V_END, Z_END, XBC_END, DT_END]
N_KEYS = 128
N_EXPERTS = N_KEYS * N_KEYS
PEER_HEADS = 8
PEER_TOPK = 16
D_KEY = 256
HALF_KEY = D_KEY // 2
EXPERT_BLOCK = 128
EPS = 1e-6

kernel_name = "hybrid_swa_ssd_peer_encoder"


def rms_norm(x, g):
    x32 = x.astype(jnp.float32)
    y = x32 * lax.rsqrt(jnp.mean(x32 * x32, axis=-1, keepdims=True) + EPS)
    return y.astype(x.dtype) * g


def alibi_slopes():
    return jnp.exp2(-8.0 * jnp.arange(1, N_Q_HEADS + 1, dtype=jnp.float32) / N_Q_HEADS)


def banded_alibi_gqa(q, k, v, sink):
    b, s = q.shape[0], q.shape[1]
    nb = s // BLOCK
    q = q.reshape(b, nb, BLOCK, N_KV_HEADS, Q_PER_KV, HEAD_DIM)
    pad = ((0, 0), (BLOCK, BLOCK), (0, 0), (0, 0))
    kp = jnp.pad(k, pad).reshape(b, nb + 2, BLOCK, N_KV_HEADS, HEAD_DIM)
    vp = jnp.pad(v, pad).reshape(b, nb + 2, BLOCK, N_KV_HEADS, HEAD_DIM)
    kw = jnp.concatenate([kp[:, :-2], kp[:, 1:-1], kp[:, 2:]], axis=2)
    vw = jnp.concatenate([vp[:, :-2], vp[:, 1:-1], vp[:, 2:]], axis=2)
    scores = jnp.einsum('bjikgd,bjmkd->bkgjim', q, kw).astype(jnp.float32) * (HEAD_DIM ** -0.5)
    qi = jnp.arange(BLOCK)[:, None]
    km = jnp.arange(3 * BLOCK)[None, :]
    rel = qi - km + BLOCK
    dist = jnp.abs(rel).astype(jnp.float32)
    band = jnp.abs(rel) <= WINDOW
    key_pos = jnp.arange(nb)[:, None] * BLOCK - BLOCK + jnp.arange(3 * BLOCK)[None, :]
    in_range = (key_pos >= 0) & (key_pos < s)
    valid = band[None, :, :] & in_range[:, None, :]
    slopes = alibi_slopes().reshape(N_KV_HEADS, Q_PER_KV)
    scores = scores - slopes[None, :, :, None, None, None] * dist
    scores = jnp.where(valid, scores, NEG_BIG)
    sink32 = sink.astype(jnp.float32).reshape(N_KV_HEADS, Q_PER_KV)[None, :, :, None, None, None]
    mx = jnp.maximum(jnp.max(scores, axis=-1, keepdims=True), sink32)
    p = jnp.exp(scores - mx)
    denom = jnp.sum(p, axis=-1, keepdims=True) + jnp.exp(sink32 - mx)
    p = (p / denom).astype(v.dtype)
    out = jnp.einsum('bkgjim,bjmkd->bjikgd', p, vw)
    return out.reshape(b, s, ATTN_W)


def centred_depthwise_conv(x, w, bias):
    c = x.shape[-1]
    half = D_CONV // 2
    y = lax.conv_general_dilated(x, w[:, None, :].astype(x.dtype), window_strides=(1,),
                                 padding=[(half, half)], dimension_numbers=('NWC', 'WIO', 'NWC'),
                                 feature_group_count=c)
    return y + bias


def ssd_chunked(x, dt, a, bm, cm):
    b, s, g, r, p = x.shape
    n = bm.shape[-1]
    nc = s // CHUNK
    f32 = jnp.float32
    xc = (x.astype(f32) * dt[..., None]).reshape(b, nc, CHUNK, g, r, p)
    bc = bm.astype(f32).reshape(b, nc, CHUNK, g, n)
    cc = cm.astype(f32).reshape(b, nc, CHUNK, g, n)
    a_cum = jnp.cumsum((dt * a).reshape(b, nc, CHUNK, g, r), axis=2)
    lower = jnp.tril(jnp.ones((CHUNK, CHUNK), dtype=bool))[None, None, :, :, None, None]
    seg = a_cum[:, :, :, None] - a_cum[:, :, None, :]
    decay_in = jnp.exp(jnp.where(lower, seg, -jnp.inf))
    cb = jnp.einsum('bclgn,bcsgn->bclsg', cc, bc)
    y_diag = jnp.einsum('bclsg,bclsgr,bcsgrp->bclgrp', cb, decay_in, xc)
    decay_to_end = jnp.exp(a_cum[:, :, -1:] - a_cum)
    states = jnp.einsum('bclgn,bclgr,bclgrp->bcgrpn', bc, decay_to_end, xc)
    chunk_decay = jnp.exp(a_cum[:, :, -1])

    def step(h, inp):
        st, dec = inp
        return h * dec[..., None, None] + st, h

    h0 = jnp.zeros((b, g, r, p, n), f32)
    _, h_prev = lax.scan(step, h0, (jnp.moveaxis(states, 1, 0), jnp.moveaxis(chunk_decay, 1, 0)))
    h_prev = jnp.moveaxis(h_prev, 0, 1)
    y_off = jnp.einsum('bclgn,bcgrpn,bclgr->bclgrp', cc, h_prev, jnp.exp(a_cum))
    return (y_diag + y_off).reshape(b, s, g, r, p)


def ssd_branch(z, xbc, dt_raw, conv_w, conv_b, a_log_f, a_log_b, dt_bias_f, dt_bias_b, d_skip, g_norm):
    b, s, _ = z.shape
    xbc = jax.nn.silu(centred_depthwise_conv(xbc, conv_w, conv_b))
    xs = xbc[..., :D_INNER].reshape(b, s, N_SSM_GROUPS, HEADS_PER_GROUP, SSM_HEAD_DIM)
    bm = xbc[..., D_INNER:D_INNER + N_SSM_GROUPS * D_STATE].reshape(b, s, N_SSM_GROUPS, D_STATE)
    cm = xbc[..., D_INNER + N_SSM_GROUPS * D_STATE:].reshape(b, s, N_SSM_GROUPS, D_STATE)
    dt32 = dt_raw.astype(jnp.float32)
    dt_f = jax.nn.softplus(dt32[..., :N_SSM_HEADS] + dt_bias_f.astype(jnp.float32)).reshape(b, s, N_SSM_GROUPS, HEADS_PER_GROUP)
    dt_b = jax.nn.softplus(dt32[..., N_SSM_HEADS:] + dt_bias_b.astype(jnp.float32)).reshape(b, s, N_SSM_GROUPS, HEADS_PER_GROUP)
    a_f = -jnp.exp(a_log_f.astype(jnp.float32)).reshape(N_SSM_GROUPS, HEADS_PER_GROUP)
    a_b = -jnp.exp(a_log_b.astype(jnp.float32)).reshape(N_SSM_GROUPS, HEADS_PER_GROUP)
    y_f = ssd_chunked(xs, dt_f, a_f, bm, cm)
    flip = lambda t: jnp.flip(t, axis=1)
    y_b = flip(ssd_chunked(flip(xs), flip(dt_b), a_b, flip(bm), flip(cm)))
    skip = d_skip.astype(jnp.float32).reshape(N_SSM_GROUPS, HEADS_PER_GROUP)[:, :, None] * xs.astype(jnp.float32)
    y = (y_f + y_b + skip).astype(z.dtype).reshape(b, s, D_INNER)
    gated = (y * jax.nn.silu(z)).reshape(b, s, N_SSM_GROUPS, D_INNER // N_SSM_GROUPS)
    return rms_norm(gated, g_norm.reshape(N_SSM_GROUPS, D_INNER // N_SSM_GROUPS)).reshape(b, s, D_INNER)


def peer_ffn(x, w_query, sub_keys, expert_u, expert_v):
    b, s, d = x.shape
    t = b * s
    xf = x.reshape(t, d)
    q = (xf @ w_query).reshape(t, PEER_HEADS, 2, HALF_KEY)
    scores = jnp.einsum('thcd,hcnd->thcn', q, sub_keys).astype(jnp.float32)
    top_s, top_i = lax.top_k(scores, PEER_TOPK)
    cand_s = (top_s[:, :, 0, :, None] + top_s[:, :, 1, None, :]).reshape(t, PEER_HEADS, PEER_TOPK * PEER_TOPK)
    cand_i = (top_i[:, :, 0, :, None] * N_KEYS + top_i[:, :, 1, None, :]).reshape(t, PEER_HEADS, PEER_TOPK * PEER_TOPK)
    best_s, pos = lax.top_k(cand_s, PEER_TOPK)
    idx = jnp.take_along_axis(cand_i, pos, axis=-1)
    gate = jax.nn.softmax(best_s, axis=-1).astype(x.dtype)
    nblk = t // EXPERT_BLOCK

    def expert_block(args):
        xb, ib, gb = args
        u = jnp.take(expert_u, ib, axis=0)
        act = jax.nn.gelu(jnp.einsum('td,thkd->thk', xb, u), approximate=False)
        vv = jnp.take(expert_v, ib, axis=0)
        return jnp.einsum('thk,thkd->td', act * gb, vv)

    out = lax.map(expert_block, (xf.reshape(nblk, EXPERT_BLOCK, d),
                                 idx.reshape(nblk, EXPERT_BLOCK, PEER_HEADS, PEER_TOPK),
                                 gate.reshape(nblk, EXPERT_BLOCK, PEER_HEADS, PEER_TOPK)))
    return out.reshape(b, s, d)


def encoder_layer(x, g_mix, w_in, attn_sink, conv_w, conv_b, a_log_f, a_log_b, dt_bias_f, dt_bias_b,
                  d_skip, g_ssm_norm, w_attn_o, w_ssm_o, w_out, g_ffn, w_query, sub_keys, expert_u, expert_v):
    b, s, _ = x.shape
    h = rms_norm(x, g_mix)
    proj = h @ w_in
    q, k, v, z, xbc, dt_raw, gates = jnp.split(proj, SPLIT_POINTS, axis=-1)
    attn = banded_alibi_gqa(q.reshape(b, s, N_Q_HEADS, HEAD_DIM), k.reshape(b, s, N_KV_HEADS, HEAD_DIM),
                            v.reshape(b, s, N_KV_HEADS, HEAD_DIM), attn_sink)
    ssm = ssd_branch(z, xbc, dt_raw, conv_w, conv_b, a_log_f, a_log_b, dt_bias_f, dt_bias_b, d_skip, g_ssm_norm)
    gate_a = jax.nn.sigmoid(gates[..., :D_MODEL])
    gate_s = jax.nn.sigmoid(gates[..., D_MODEL:])
    merged = gate_a * (attn @ w_attn_o) + gate_s * (ssm @ w_ssm_o)
    x = x + merged @ w_out
    x = x + peer_ffn(rms_norm(x, g_ffn), w_query, sub_keys, expert_u, expert_v)
    return x


def trunk(x, g_mix, w_in, attn_sink, conv_w, conv_b, a_log_f, a_log_b, dt_bias_f, dt_bias_b, d_skip,
          g_ssm_norm, w_attn_o, w_ssm_o, w_out, g_ffn, w_query, sub_keys, expert_u, expert_v, g_final):
    for l in range(DEPTH):
        x = encoder_layer(x, g_mix[l], w_in[l], attn_sink[l], conv_w[l], conv_b[l], a_log_f[l], a_log_b[l],
                          dt_bias_f[l], dt_bias_b[l], d_skip[l], g_ssm_norm[l], w_attn_o[l], w_ssm_o[l],
                          w_out[l], g_ffn[l], w_query[l], sub_keys[l], expert_u[l], expert_v[l])
    return rms_norm(x, g_final)


def setup_inputs(seed: int = 0) -> dict:
    key = jax.random.key(seed)
    ks = jax.random.split(key, 24)
    f32 = jnp.float32
    nrm = lambda k, shape, scale: jax.random.normal(k, shape, f32) * scale

    def dt_bias(k):
        dt = jnp.exp(jax.random.uniform(k, (DEPTH, N_SSM_HEADS), f32, math.log(1e-3), math.log(1e-1)))
        return dt + jnp.log(-jnp.expm1(-dt))

    col_scale = jnp.ones((IN_W,), f32).at[DT_START:DT_END].set(0.1)
    return {
        "x_prompt": nrm(ks[0], (BATCH, SEQ, D_MODEL), 1.0),
        "x_sample": nrm(ks[1], (DEC_BATCH, DEC_SEQ, D_MODEL), 1.0),
        "g_mix": 1.0 + nrm(ks[2], (DEPTH, D_MODEL), 0.02),
        "w_in": nrm(ks[3], (DEPTH, D_MODEL, IN_W), D_MODEL ** -0.5) * col_scale,
        "attn_sink": nrm(ks[4], (DEPTH, N_Q_HEADS), 0.5),
        "conv_w": nrm(ks[5], (DEPTH, D_CONV, XBC_W), D_CONV ** -0.5),
        "conv_b": nrm(ks[6], (DEPTH, XBC_W), 0.02),
        "a_log_f": jnp.log(jax.random.uniform(ks[7], (DEPTH, N_SSM_HEADS), f32, 1.0, 16.0)),
        "a_log_b": jnp.log(jax.random.uniform(ks[8], (DEPTH, N_SSM_HEADS), f32, 1.0, 16.0)),
        "dt_bias_f": dt_bias(ks[9]),
        "dt_bias_b": dt_bias(ks[10]),
        "d_skip": 1.0 + nrm(ks[11], (DEPTH, N_SSM_HEADS), 0.1),
        "g_ssm_norm": 1.0 + nrm(ks[12], (DEPTH, D_INNER), 0.02),
        "w_attn_o": nrm(ks[13], (DEPTH, ATTN_W, D_MODEL), ATTN_W ** -0.5),
        "w_ssm_o": nrm(ks[14], (DEPTH, D_INNER, D_MODEL), D_INNER ** -0.5),
        "w_out": nrm(ks[15], (DEPTH, D_MODEL, D_MODEL), D_MODEL ** -0.5),
        "g_ffn": 1.0 + nrm(ks[16], (DEPTH, D_MODEL), 0.02),
        "w_query": nrm(ks[17], (DEPTH, D_MODEL, PEER_HEADS * D_KEY), D_MODEL ** -0.5),
        "sub_keys": nrm(ks[18], (DEPTH, PEER_HEADS, 2, N_KEYS, HALF_KEY), HALF_KEY ** -0.5),
        "expert_u": nrm(ks[19], (DEPTH, N_EXPERTS, D_MODEL), D_MODEL ** -0.5),
        "expert_v": nrm(ks[20], (DEPTH, N_EXPERTS, D_MODEL), 0.25),
        "g_final": 1.0 + nrm(ks[21], (D_MODEL,), 0.02),
    }


def reference(x_prompt, x_sample, g_mix, w_in, attn_sink, conv_w, conv_b, a_log_f, a_log_b, dt_bias_f,
              dt_bias_b, d_skip, g_ssm_norm, w_attn_o, w_ssm_o, w_out, g_ffn, w_query, sub_keys,
              expert_u, expert_v, g_final):
    y_prompt = trunk(x_prompt, g_mix, w_in, attn_sink, conv_w, conv_b, a_log_f, a_log_b, dt_bias_f, dt_bias_b,
                     d_skip, g_ssm_norm, w_attn_o, w_ssm_o, w_out, g_ffn, w_query, sub_keys, expert_u,
                     expert_v, g_final)
    y_sample = trunk(x_sample, g_mix, w_in, attn_sink, conv_w, conv_b, a_log_f, a_log_b, dt_bias_f, dt_bias_b,
                     d_skip, g_ssm_norm, w_attn_o, w_ssm_o, w_out, g_ffn, w_query, sub_keys, expert_u,
                     expert_v, g_final)
    return (y_prompt, y_sample)
```

```python
import functools
import math

import jax
import jax.numpy as jnp
from jax import lax
from jax.experimental import pallas as pl
from jax.experimental.pallas import tpu as pltpu

F32 = jnp.float32
BF16 = jnp.bfloat16

D_MODEL = 2048
N_Q_HEADS = 16
N_KV_HEADS = 4
HEAD_DIM = 64
Q_PER_KV = N_Q_HEADS // N_KV_HEADS
WINDOW = 128
BLOCK = 128
ATTN_W = N_Q_HEADS * HEAD_DIM
KV_W = N_KV_HEADS * HEAD_DIM
NEG_BIG = -1e30
D_INNER = D_MODEL
SSM_HEAD_DIM = 64
N_SSM_HEADS = D_INNER // SSM_HEAD_DIM
N_SSM_GROUPS = 4
D_STATE = 128
D_CONV = 5
CHUNK = 128
XBC_W = D_INNER + 2 * N_SSM_GROUPS * D_STATE
N_KEYS = 128
N_EXPERTS = N_KEYS * N_KEYS
PEER_HEADS = 8
PEER_TOPK = 16
HALF_KEY = 128
EPS = 1e-6

_Q_END = ATTN_W
_K_END = _Q_END + KV_W
_V_END = _K_END + KV_W
_Z_END = _V_END + D_INNER
_XBC_END = _Z_END + XBC_W
_DT_END = _XBC_END + 2 * N_SSM_HEADS
_IN_W = _DT_END + 2 * D_MODEL

LANES = 128
P_Z = 0
P_GA = P_Z + D_INNER
P_GS = P_GA + D_MODEL
P_XBC = P_GS + D_MODEL
P_Q = P_XBC + XBC_W
P_K = P_Q + ATTN_W
P_V = P_K + KV_W
P_DT = P_V + KV_W
P_W = P_DT + LANES

VMEM_LIMIT = 56 * 1024 * 1024


def _cparams(sem):
    return pltpu.CompilerParams(dimension_semantics=sem, vmem_limit_bytes=VMEM_LIMIT)


def _sigmoid(x):
    return 1.0 / (1.0 + jnp.exp(-x))


def _rms(x, g):
    ms = jnp.mean(x * x, axis=-1, keepdims=True)
    return (x * lax.rsqrt(ms + EPS)) * g


def _inproj_kernel(x_ref, g_ref, w_ref, o_ref, h_scr):
    @pl.when(pl.program_id(1) == 0)
    def _():
        h_scr[...] = _rms(x_ref[...], g_ref[...]).astype(BF16)

    o_ref[...] = jnp.dot(h_scr[...], w_ref[...], preferred_element_type=F32)


def _inproj(x2, g_mix, w_packed, tm=512, tn=2176):
    t = x2.shape[0]
    return pl.pallas_call(
        _inproj_kernel,
        grid=(t // tm, P_W // tn),
        in_specs=[
            pl.BlockSpec((tm, D_MODEL), lambda i, j: (i, 0)),
            pl.BlockSpec((1, D_MODEL), lambda i, j: (0, 0)),
            pl.BlockSpec((D_MODEL, tn), lambda i, j: (0, j)),
        ],
        out_specs=pl.BlockSpec((tm, tn), lambda i, j: (i, j)),
        out_shape=jax.ShapeDtypeStruct((t, P_W), F32),
        scratch_shapes=[pltpu.VMEM((tm, D_MODEL), BF16)],
        compiler_params=_cparams(("parallel", "arbitrary")),
        name="inproj",
    )(x2, g_mix, w_packed)


def _attn_kernel(hp_ref, q_ref, kp_ref, kc_ref, kn_ref, vp_ref, vc_ref, vn_ref, o_ref, *, seq):
    j = pl.program_id(1)
    q = q_ref[...].astype(BF16)
    kw = jnp.concatenate([kp_ref[...], kc_ref[...], kn_ref[...]], axis=0).astype(BF16)
    vw = jnp.concatenate([vp_ref[...], vc_ref[...], vn_ref[...]], axis=0).astype(BF16)
    qi = lax.broadcasted_iota(jnp.int32, (BLOCK, 3 * BLOCK), 0)
    km = lax.broadcasted_iota(jnp.int32, (BLOCK, 3 * BLOCK), 1)
    rel = qi - km + BLOCK
    dist = jnp.abs(rel).astype(F32)
    key_pos = (j - 1) * BLOCK + km
    valid = (jnp.abs(rel) <= WINDOW) & (key_pos >= 0) & (key_pos < seq)
    outs = []
    for h in range(N_Q_HEADS):
        kv = h // Q_PER_KV
        qh = q[:, h * HEAD_DIM:(h + 1) * HEAD_DIM]
        kh = kw[:, kv * HEAD_DIM:(kv + 1) * HEAD_DIM]
        vh = vw[:, kv * HEAD_DIM:(kv + 1) * HEAD_DIM]
        s = lax.dot_general(qh, kh, (((1,), (1,)), ((), ())), preferred_element_type=F32)
        s = s * (HEAD_DIM ** -0.5) - hp_ref[0, h] * dist
        s = jnp.where(valid, s, NEG_BIG)
        sink = hp_ref[1, h]
        mx = jnp.maximum(jnp.max(s, axis=-1, keepdims=True), sink)
        p = jnp.exp(s - mx)
        denom = jnp.sum(p, axis=-1, keepdims=True) + jnp.exp(sink - mx)
        p = (p / denom).astype(BF16)
        outs.append(jnp.dot(p, vh, preferred_element_type=F32))
    o_ref[...] = jnp.concatenate(outs, axis=-1).astype(o_ref.dtype)


def _attention(proj3, head_params):
    b, s, _ = proj3.shape
    nb = s // BLOCK
    kcol, vcol = P_K // KV_W, P_V // KV_W
    prev = lambda bb, j: jnp.maximum(j - 1, 0)
    nxt = lambda bb, j: jnp.minimum(j + 1, nb - 1)
    kv_spec = lambda col, f: pl.BlockSpec((None, BLOCK, KV_W), lambda bb, j: (bb, f(bb, j), col))
    cur = lambda bb, j: j
    return pl.pallas_call(
        functools.partial(_attn_kernel, seq=s),
        grid=(b, nb),
        in_specs=[
            pl.BlockSpec(memory_space=pltpu.SMEM),
            pl.BlockSpec((None, BLOCK, ATTN_W), lambda bb, j: (bb, j, P_Q // ATTN_W)),
            kv_spec(kcol, prev), kv_spec(kcol, cur), kv_spec(kcol, nxt),
            kv_spec(vcol, prev), kv_spec(vcol, cur), kv_spec(vcol, nxt),
        ],
        out_specs=pl.BlockSpec((None, BLOCK, ATTN_W), lambda bb, j: (bb, j, 0)),
        out_shape=jax.ShapeDtypeStruct((b, s, ATTN_W), BF16),
        compiler_params=_cparams(("parallel", "arbitrary")),
        name="attn",
    )(head_params, proj3, proj3, proj3, proj3, proj3, proj3, proj3)


HALO = 8
CONV_TB = 512
CONV_CB = 512


def _conv_kernel(cur_ref, prev_ref, next_ref, w_ref, b_ref, o_ref, ext):
    i = pl.program_id(1)
    last = pl.num_programs(1) - 1
    tb = cur_ref.shape[0]
    ext[0:HALO, :] = jnp.where(i == 0, 0.0, prev_ref[...])
    ext[HALO:HALO + tb, :] = cur_ref[...]
    ext[HALO + tb:, :] = jnp.where(i == last, 0.0, next_ref[...])
    half = D_CONV // 2
    acc = b_ref[...] + w_ref[0:1, :] * ext[HALO - half:HALO - half + tb, :]
    for k in range(1, D_CONV):
        acc = acc + w_ref[k:k + 1, :] * ext[HALO - half + k:HALO - half + k + tb, :]
    o_ref[...] = acc * _sigmoid(acc)


def _conv_silu(proj3, conv_w, conv_b):
    b, s, _ = proj3.shape
    tb, cb = CONV_TB, CONV_CB
    c0 = P_XBC // cb
    nrow8 = s // HALO
    return pl.pallas_call(
        _conv_kernel,
        grid=(b, s // tb, XBC_W // cb),
        in_specs=[
            pl.BlockSpec((None, tb, cb), lambda bb, i, c: (bb, i, c0 + c)),
            pl.BlockSpec((None, HALO, cb), lambda bb, i, c: (bb, jnp.maximum(i * (tb // HALO) - 1, 0), c0 + c)),
            pl.BlockSpec((None, HALO, cb),
                         lambda bb, i, c: (bb, jnp.minimum((i + 1) * (tb // HALO), nrow8 - 1), c0 + c)),
            pl.BlockSpec((D_CONV, cb), lambda bb, i, c: (0, c)),
            pl.BlockSpec((1, cb), lambda bb, i, c: (0, c)),
        ],
        out_specs=pl.BlockSpec((None, tb, cb), lambda bb, i, c: (bb, i, c)),
        out_shape=jax.ShapeDtypeStruct((b, s, XBC_W), F32),
        scratch_shapes=[pltpu.VMEM((tb + 2 * HALO, cb), F32)],
        compiler_params=_cparams(("parallel", "arbitrary", "arbitrary")),
        name="conv",
    )(proj3, proj3, proj3, conv_w, conv_b)


N_PAIRS = N_SSM_HEADS // 2
PAIRS_PER_GROUP = N_PAIRS // N_SSM_GROUPS
GROUP_W = D_INNER // N_SSM_GROUPS


def _softplus(x):
    return jnp.maximum(x, 0.0) + jnp.log1p(jnp.exp(-jnp.abs(x)))


def _ssd_kernel(*refs, reverse):
    if reverse:
        xbc_ref, dt_ref, alog_ref, dtb_ref, yf_ref, z_ref, dskip_ref, gn_ref, o_ref, state, ybuf = refs
    else:
        xbc_ref, dt_ref, alog_ref, dtb_ref, o_ref, state = refs
        ybuf = o_ref
    L = CHUNK

    @pl.when(pl.program_id(1) == 0)
    def _():
        state[...] = jnp.zeros_like(state)

    dt = _softplus(dt_ref[...] + dtb_ref[...])
    dta = dt * (-jnp.exp(alog_ref[...]))
    row = lax.broadcasted_iota(jnp.int32, (L, L), 0)
    col = lax.broadcasted_iota(jnp.int32, (L, L), 1)
    tri = (col >= row) if reverse else (col <= row)
    cum = jnp.dot(tri.astype(F32), dta, precision=lax.Precision.HIGHEST, preferred_element_type=F32)
    cum_t = cum.T
    end = 0 if reverse else L - 1
    ecum = jnp.exp(cum)
    left = lax.broadcasted_iota(jnp.int32, (L, LANES), 1) < SSM_HEAD_DIM
    off = N_SSM_HEADS if reverse else 0

    for g in range(N_SSM_GROUPS):
        bg = xbc_ref[:, D_INNER + g * D_STATE:D_INNER + (g + 1) * D_STATE]
        cg = xbc_ref[:, D_INNER + (N_SSM_GROUPS + g) * D_STATE:D_INNER + (N_SSM_GROUPS + g + 1) * D_STATE]
        cb = lax.dot_general(cg.astype(BF16), bg.astype(BF16), (((1,), (1,)), ((), ())),
                             preferred_element_type=F32)
        bt = bg.T
        for pp in range(PAIRS_PER_GROUP):
            hp = g * PAIRS_PER_GROUP + pp
            h1, h2 = 2 * hp + off, 2 * hp + 1 + off
            xp = xbc_ref[:, hp * LANES:(hp + 1) * LANES]
            dtp = jnp.where(left, dt[:, h1:h1 + 1], dt[:, h2:h2 + 1])
            xdt = (xp * dtp).astype(BF16)
            zero = jnp.zeros_like(xdt)
            xbd = jnp.concatenate([jnp.where(left, xdt, zero), jnp.where(left, zero, xdt)], axis=0)
            ms, cds, bds, decs = [], [], [], []
            for h in (h1, h2):
                seg = cum[:, h:h + 1] - cum_t[h:h + 1, :]
                lmat = jnp.exp(jnp.where(tri, seg, NEG_BIG))
                ms.append((cb * lmat).astype(BF16))
                cds.append((cg * ecum[:, h:h + 1]).astype(BF16))
                tot = cum_t[h:h + 1, end:end + 1]
                bds.append((bt * jnp.exp(tot - cum_t[h:h + 1, :])).astype(BF16))
                decs.append(jnp.exp(tot))
            sp = state[hp]
            spb = sp.astype(BF16)
            zs = jnp.zeros_like(spb)
            sbd = jnp.concatenate([jnp.where(left, spb, zs), jnp.where(left, zs, spb)], axis=0)
            y = jnp.dot(jnp.concatenate(ms, axis=1), xbd, preferred_element_type=F32)
            y = y + jnp.dot(jnp.concatenate(cds, axis=1), sbd, preferred_element_type=F32)
            ybuf[:, hp * LANES:(hp + 1) * LANES] = y
            snew = jnp.dot(jnp.concatenate(bds, axis=1), xbd, preferred_element_type=F32)
            state[hp] = sp * jnp.where(left, decs[0], decs[1]) + snew

    if reverse:
        y = ybuf[...] + yf_ref[...] + dskip_ref[...] * xbc_ref[:, 0:D_INNER]
        z = z_ref[...]
        gated = y * (z * _sigmoid(z))
        for g in range(N_SSM_GROUPS):
            sl = slice(g * GROUP_W, (g + 1) * GROUP_W)
            o_ref[:, sl] = _rms(gated[:, sl], gn_ref[:, sl]).astype(o_ref.dtype)


def _ssd(xbc_act, proj3, alog_all, dtb_all, reverse, extras=()):
    b, s, _ = xbc_act.shape
    nc = s // CHUNK
    cidx = (lambda c: nc - 1 - c) if reverse else (lambda c: c)
    row_spec = lambda w, colblk: pl.BlockSpec((None, CHUNK, w), lambda bb, c: (bb, cidx(c), colblk))
    par_spec = lambda w: pl.BlockSpec((1, w), lambda bb, c: (0, 0))
    in_specs = [row_spec(XBC_W, 0), row_spec(LANES, P_DT // LANES), par_spec(LANES), par_spec(LANES)]
    args = [xbc_act, proj3, alog_all, dtb_all]
    scratch = [pltpu.VMEM((N_PAIRS, D_STATE, LANES), F32)]
    if reverse:
        yf, dskip, gnorm = extras
        in_specs += [row_spec(D_INNER, 0), row_spec(D_INNER, P_Z // D_INNER), par_spec(D_INNER), par_spec(D_INNER)]
        args += [yf, proj3, dskip, gnorm]
        scratch.append(pltpu.VMEM((CHUNK, D_INNER), F32))
        out_dtype = BF16
    else:
        out_dtype = F32
    return pl.pallas_call(
        functools.partial(_ssd_kernel, reverse=reverse),
        grid=(b, nc),
        in_specs=in_specs,
        out_specs=row_spec(D_INNER, 0),
        out_shape=jax.ShapeDtypeStruct((b, s, D_INNER), out_dtype),
        scratch_shapes=scratch,
        compiler_params=_cparams(("parallel", "arbitrary")),
        name="ssd_bwd" if reverse else "ssd_fwd",
    )(*args)


def _merge_kernel(attn_ref, ssm_ref, ga_ref, gs_ref, x_ref, wa_ref, ws_ref, wo_ref, gf_ref, x1_ref, hn_ref):
    a = jnp.dot(attn_ref[...], wa_ref[...], preferred_element_type=F32)
    s = jnp.dot(ssm_ref[...], ws_ref[...], preferred_element_type=F32)
    merged = _sigmoid(ga_ref[...]) * a + _sigmoid(gs_ref[...]) * s
    x1 = x_ref[...] + jnp.dot(merged.astype(BF16), wo_ref[...], preferred_element_type=F32)
    x1_ref[...] = x1
    hn_ref[...] = _rms(x1, gf_ref[...]).astype(BF16)


def _resident(shape):
    return pl.BlockSpec(shape, lambda *_: (0,) * len(shape), pipeline_mode=pl.Buffered(1))


def _merge(attn2, ssm2, proj2, x2, wa, ws, wo, g_ffn, tm=256):
    t = x2.shape[0]
    rows = lambda w, colblk=0: pl.BlockSpec((tm, w), lambda i: (i, colblk))
    return pl.pallas_call(
        _merge_kernel,
        grid=(t // tm,),
        in_specs=[
            rows(ATTN_W), rows(D_INNER), rows(D_MODEL, P_GA // D_MODEL), rows(D_MODEL, P_GS // D_MODEL),
            rows(D_MODEL), _resident((ATTN_W, D_MODEL)), _resident((D_INNER, D_MODEL)),
            _resident((D_MODEL, D_MODEL)), _resident((1, D_MODEL)),
        ],
        out_specs=[rows(D_MODEL), rows(D_MODEL)],
        out_shape=[jax.ShapeDtypeStruct((t, D_MODEL), F32), jax.ShapeDtypeStruct((t, D_MODEL), BF16)],
        compiler_params=_cparams(("parallel",)),
        name="merge",
    )(attn2, ssm2, proj2, proj2, x2, wa, ws, wo, g_ffn)


def _extract_topk(s, k_top):
    r = s.shape[0]
    iota = lax.broadcasted_iota(jnp.int32, s.shape, 0)
    pos = jnp.full(s.shape, float(k_top), F32)
    vals, idxs = [], []
    for k in range(k_top):
        m = jnp.max(s, axis=0, keepdims=True)
        idx = jnp.min(jnp.where(s == m, iota, r), axis=0, keepdims=True)
        hit = iota == idx
        pos = jnp.where(hit, float(k), pos)
        s = jnp.where(hit, -jnp.inf, s)
        vals.append(m)
        idxs.append(idx)
    return vals, pos, idxs


def _route_kernel(hn_ref, wq_ref, sk_ref, cnt_ref, a_ref, pos_ref, b_ref):
    k_top = PEER_TOPK
    q = jnp.dot(hn_ref[...], wq_ref[...], preferred_element_type=F32).astype(BF16)
    n = q.shape[0]
    iota_k = lax.broadcasted_iota(jnp.int32, (k_top, n), 0)
    for h in range(PEER_HEADS):
        sc, vals, poss = [], [], []
        for c in range(2):
            qhc = q[:, (2 * h + c) * HALF_KEY:(2 * h + c + 1) * HALF_KEY]
            s = lax.dot_general(sk_ref[2 * h + c], qhc, (((1,), (1,)), ((), ())), preferred_element_type=F32)
            v, p, _ = _extract_topk(s, k_top)
            sc.append(s)
            vals.append(v)
            poss.append(p)
        v2 = jnp.zeros((k_top, n), F32)
        for k in range(k_top):
            v2 = jnp.where(iota_k == k, vals[1][k], v2)
        cand = jnp.concatenate([vals[0][k] + v2 for k in range(k_top)], axis=0)
        best, _, idxs = _extract_topk(cand, k_top)
        cnt16 = jnp.zeros((k_top, n), F32)
        zsum = jnp.zeros((1, n), F32)
        for k in range(k_top):
            cnt16 = cnt16 + jnp.where(iota_k == idxs[k] // k_top, 1.0, 0.0)
            zsum = zsum + jnp.exp(best[k] - best[0])
        cnt = jnp.zeros((N_KEYS, n), F32)
        for k in range(k_top):
            cnt = jnp.where(poss[0] == float(k), cnt16[k:k + 1, :], cnt)
        cnt_ref[h] = cnt
        a_ref[h] = jnp.exp(sc[0] - vals[0][0])
        pos_ref[h] = poss[1]
        b_ref[h] = jnp.exp(sc[1] - vals[1][0]) / zsum


def _route(hn2, wq, sk, tm=256):
    t = hn2.shape[0]
    out = jax.ShapeDtypeStruct((PEER_HEADS, N_KEYS, t), F32)
    ospec = pl.BlockSpec((PEER_HEADS, N_KEYS, tm), lambda i: (0, 0, i))
    return pl.pallas_call(
        _route_kernel,
        grid=(t // tm,),
        in_specs=[
            pl.BlockSpec((tm, D_MODEL), lambda i: (i, 0)),
            _resident((D_MODEL, 2 * PEER_HEADS * HALF_KEY)),
            _resident((2 * PEER_HEADS, N_KEYS, HALF_KEY)),
        ],
        out_specs=[ospec] * 4,
        out_shape=[out] * 4,
        compiler_params=_cparams(("parallel",)),
        name="route",
    )(hn2, wq, sk)


SQRT_HALF = math.sqrt(0.5)


def _peer_kernel(hn_ref, u_ref, vt_ref, cnt_ref, a_ref, pos_ref, b_ref, x1_ref, gf_ref, y_ref, acc_ref):
    j = pl.program_id(1)
    te = u_ref.shape[0]
    rows_per_tile = te // N_KEYS

    @pl.when(j == 0)
    def _():
        acc_ref[...] = jnp.zeros_like(acc_ref)

    ht = lax.dot_general(u_ref[...], hn_ref[...], (((1,), (1,)), ((), ())), preferred_element_type=F32)
    pieces = []
    for r in range(rows_per_tile):
        i1 = j * rows_per_tile + r
        w = jnp.zeros((N_KEYS, ht.shape[1]), F32)
        for h in range(PEER_HEADS):
            cnt_row = cnt_ref[h, pl.ds(i1, 1), :]
            a_row = a_ref[h, pl.ds(i1, 1), :]
            w = w + jnp.where(pos_ref[h] < cnt_row, b_ref[h] * a_row, 0.0)
        hr = ht[r * N_KEYS:(r + 1) * N_KEYS, :]
        act = 0.5 * hr * (1.0 + lax.erf(hr * SQRT_HALF))
        pieces.append((act * w).astype(BF16))
    p = jnp.concatenate(pieces, axis=0)
    acc_ref[...] += jnp.dot(vt_ref[...], p, preferred_element_type=F32)

    @pl.when(j == pl.num_programs(1) - 1)
    def _():
        y_ref[...] = _rms(x1_ref[...] + acc_ref[...].T, gf_ref[...])


def _peer(hn2, u_bf, vt_bf, route, x1, g_final, tm=512, te=512):
    t = hn2.shape[0]
    rspec = pl.BlockSpec((PEER_HEADS, N_KEYS, tm), lambda i, j: (0, 0, i))
    return pl.pallas_call(
        _peer_kernel,
        grid=(t // tm, N_EXPERTS // te),
        in_specs=[
            pl.BlockSpec((tm, D_MODEL), lambda i, j: (i, 0)),
            pl.BlockSpec((te, D_MODEL), lambda i, j: (j, 0)),
            pl.BlockSpec((D_MODEL, te), lambda i, j: (0, j)),
            rspec, rspec, rspec, rspec,
            pl.BlockSpec((tm, D_MODEL), lambda i, j: (i, 0)),
            pl.BlockSpec((1, D_MODEL), lambda i, j: (0, 0)),
        ],
        out_specs=pl.BlockSpec((tm, D_MODEL), lambda i, j: (i, 0)),
        out_shape=jax.ShapeDtypeStruct((t, D_MODEL), F32),
        scratch_shapes=[pltpu.VMEM((D_MODEL, tm), F32)],
        compiler_params=_cparams(("parallel", "arbitrary")),
        name="peer",
    )(hn2, u_bf, vt_bf, *route, x1, g_final)


def _pad_lanes(v):
    return jnp.pad(v, (0, LANES - v.shape[0])).reshape(1, LANES)


def _prep(g_mix, w_in, attn_sink, conv_w, conv_b, a_log_f, a_log_b, dt_bias_f, dt_bias_b, d_skip, g_ssm_norm,
          w_attn_o, w_ssm_o, w_out, g_ffn, w_query, sub_keys, expert_u, expert_v, g_final):
    w = w_in[0]
    w_packed = jnp.concatenate(
        [w[:, _V_END:_Z_END], w[:, _DT_END:_IN_W], w[:, _Z_END:_XBC_END], w[:, :_V_END], w[:, _XBC_END:_DT_END],
         jnp.zeros((D_MODEL, LANES - 2 * N_SSM_HEADS), w.dtype)], axis=1).astype(BF16)
    slopes = jnp.exp2(-8.0 * jnp.arange(1, N_Q_HEADS + 1, dtype=F32) / N_Q_HEADS)
    return dict(
        g_mix=g_mix.reshape(1, D_MODEL),
        w_packed=w_packed,
        head_params=jnp.stack([slopes, attn_sink[0].astype(F32)]),
        conv_w=conv_w[0], conv_b=conv_b.reshape(1, XBC_W),
        alog_all=_pad_lanes(jnp.concatenate([a_log_f[0], a_log_b[0]])),
        dtb_all=_pad_lanes(jnp.concatenate([dt_bias_f[0], dt_bias_b[0]])),
        dskip=jnp.repeat(d_skip[0], SSM_HEAD_DIM).reshape(1, D_INNER),
        gnorm=g_ssm_norm.reshape(1, D_INNER),
        wa=w_attn_o[0].astype(BF16), ws=w_ssm_o[0].astype(BF16), wo=w_out[0].astype(BF16),
        g_ffn=g_ffn.reshape(1, D_MODEL),
        wq=w_query[0].astype(BF16),
        sk=sub_keys[0].reshape(2 * PEER_HEADS, N_KEYS, HALF_KEY).astype(BF16),
        u_bf=expert_u[0].astype(BF16),
        vt_bf=expert_v[0].T.astype(BF16),
        g_final=g_final.reshape(1, D_MODEL),
    )


def _trunk(x, p):
    b, s, d = x.shape
    t = b * s
    x2 = x.reshape(t, d)
    proj2 = _inproj(x2, p["g_mix"], p["w_packed"])
    proj3 = proj2.reshape(b, s, P_W)
    attn = _attention(proj3, p["head_params"])
    xbc_act = _conv_silu(proj3, p["conv_w"], p["conv_b"])
    y_f = _ssd(xbc_act, proj3, p["alog_all"], p["dtb_all"], reverse=False)
    ssm = _ssd(xbc_act, proj3, p["alog_all"], p["dtb_all"], reverse=True, extras=(y_f, p["dskip"], p["gnorm"]))
    x1, hn = _merge(attn.reshape(t, ATTN_W), ssm.reshape(t, D_INNER), proj2, x2,
                    p["wa"], p["ws"], p["wo"], p["g_ffn"])
    route = _route(hn, p["wq"], p["sk"])
    y = _peer(hn, p["u_bf"], p["vt_bf"], route, x1, p["g_final"])
    return y.reshape(b, s, d)


def kernel(x_prompt, x_sample, g_mix, w_in, attn_sink, conv_w, conv_b, a_log_f, a_log_b, dt_bias_f, dt_bias_b,
           d_skip, g_ssm_norm, w_attn_o, w_ssm_o, w_out, g_ffn, w_query, sub_keys, expert_u, expert_v, g_final):
    p = _prep(g_mix, w_in, attn_sink, conv_w, conv_b, a_log_f, a_log_b, dt_bias_f, dt_bias_b, d_skip, g_ssm_norm,
              w_attn_o, w_ssm_o, w_out, g_ffn, w_query, sub_keys, expert_u, expert_v, g_final)
    return (_trunk(x_prompt, p), _trunk(x_sample, p))
```

```python
import functools
import math

import jax
import jax.numpy as jnp
from jax import lax
from jax.experimental import pallas as pl
from jax.experimental.pallas import tpu as pltpu

F32 = jnp.float32
BF16 = jnp.bfloat16

D_MODEL = 2048
N_Q_HEADS = 16
N_KV_HEADS = 4
HEAD_DIM = 64
Q_PER_KV = N_Q_HEADS // N_KV_HEADS
WINDOW = 128
BLOCK = 128
ATTN_W = N_Q_HEADS * HEAD_DIM
KV_W = N_KV_HEADS * HEAD_DIM
NEG_BIG = -1e30
D_INNER = D_MODEL
SSM_HEAD_DIM = 64
N_SSM_HEADS = D_INNER // SSM_HEAD_DIM
N_SSM_GROUPS = 4
D_STATE = 128
D_CONV = 5
CHUNK = 128
XBC_W = D_INNER + 2 * N_SSM_GROUPS * D_STATE
N_KEYS = 128
N_EXPERTS = N_KEYS * N_KEYS
PEER_HEADS = 8
PEER_TOPK = 16
HALF_KEY = 128
EPS = 1e-6

_Q_END = ATTN_W
_K_END = _Q_END + KV_W
_V_END = _K_END + KV_W
_Z_END = _V_END + D_INNER
_XBC_END = _Z_END + XBC_W
_DT_END = _XBC_END + 2 * N_SSM_HEADS
_IN_W = _DT_END + 2 * D_MODEL

LANES = 128
P_Z = 0
P_GA = P_Z + D_INNER
P_GS = P_GA + D_MODEL
P_XBC = P_GS + D_MODEL
P_Q = P_XBC + XBC_W
P_K = P_Q + ATTN_W
P_V = P_K + KV_W
P_DT = P_V + KV_W
P_W = P_DT + LANES

VMEM_LIMIT = 56 * 1024 * 1024


def _cparams(sem, flags=None):
    return pltpu.CompilerParams(dimension_semantics=sem, vmem_limit_bytes=VMEM_LIMIT, flags=flags)


def _sigmoid(x):
    return 1.0 / (1.0 + jnp.exp(-x))


def _rms(x, g):
    ms = jnp.mean(x * x, axis=-1, keepdims=True)
    return (x * lax.rsqrt(ms + EPS)) * g


def _inproj_kernel(x_ref, g_ref, w_ref, o_ref, h_scr):
    @pl.when(pl.program_id(1) == 0)
    def _():
        h_scr[...] = _rms(x_ref[...], g_ref[...]).astype(BF16)

    o_ref[...] = jnp.dot(h_scr[...], w_ref[...], preferred_element_type=F32)


def _inproj(x2, g_mix, w_packed, tm=512, tn=2176):
    t = x2.shape[0]
    return pl.pallas_call(
        _inproj_kernel,
        grid=(t // tm, P_W // tn),
        in_specs=[
            pl.BlockSpec((tm, D_MODEL), lambda i, j: (i, 0)),
            pl.BlockSpec((1, D_MODEL), lambda i, j: (0, 0)),
            pl.BlockSpec((D_MODEL, tn), lambda i, j: (0, j)),
        ],
        out_specs=pl.BlockSpec((tm, tn), lambda i, j: (i, j)),
        out_shape=jax.ShapeDtypeStruct((t, P_W), F32),
        scratch_shapes=[pltpu.VMEM((tm, D_MODEL), BF16)],
        compiler_params=_cparams(("parallel", "arbitrary")),
        name="inproj",
    )(x2, g_mix, w_packed)


def _attn_kernel(hp_ref, q_ref, kp_ref, kc_ref, kn_ref, vp_ref, vc_ref, vn_ref, o_ref, *, seq):
    j = pl.program_id(1)
    q = q_ref[...].astype(BF16)
    kw = jnp.concatenate([kp_ref[...], kc_ref[...], kn_ref[...]], axis=0).astype(BF16)
    vw = jnp.concatenate([vp_ref[...], vc_ref[...], vn_ref[...]], axis=0).astype(BF16)
    qi = lax.broadcasted_iota(jnp.int32, (BLOCK, 3 * BLOCK), 0)
    km = lax.broadcasted_iota(jnp.int32, (BLOCK, 3 * BLOCK), 1)
    rel = qi - km + BLOCK
    dist = jnp.abs(rel).astype(F32)
    key_pos = (j - 1) * BLOCK + km
    valid = (jnp.abs(rel) <= WINDOW) & (key_pos >= 0) & (key_pos < seq)
    outs = []
    for h in range(N_Q_HEADS):
        kv = h // Q_PER_KV
        qh = q[:, h * HEAD_DIM:(h + 1) * HEAD_DIM]
        kh = kw[:, kv * HEAD_DIM:(kv + 1) * HEAD_DIM]
        vh = vw[:, kv * HEAD_DIM:(kv + 1) * HEAD_DIM]
        s = lax.dot_general(qh, kh, (((1,), (1,)), ((), ())), preferred_element_type=F32)
        s = s * (HEAD_DIM ** -0.5) - hp_ref[0, h] * dist
        s = jnp.where(valid, s, NEG_BIG)
        sink = hp_ref[1, h]
        mx = jnp.maximum(jnp.max(s, axis=-1, keepdims=True), sink)
        p = jnp.exp(s - mx)
        denom = jnp.sum(p, axis=-1, keepdims=True) + jnp.exp(sink - mx)
        p = (p / denom).astype(BF16)
        outs.append(jnp.dot(p, vh, preferred_element_type=F32))
    o_ref[...] = jnp.concatenate(outs, axis=-1).astype(o_ref.dtype)


def _attention(proj3, head_params):
    b, s, _ = proj3.shape
    nb = s // BLOCK
    kcol, vcol = P_K // KV_W, P_V // KV_W
    prev = lambda bb, j: jnp.maximum(j - 1, 0)
    nxt = lambda bb, j: jnp.minimum(j + 1, nb - 1)
    kv_spec = lambda col, f: pl.BlockSpec((None, BLOCK, KV_W), lambda bb, j: (bb, f(bb, j), col))
    cur = lambda bb, j: j
    return pl.pallas_call(
        functools.partial(_attn_kernel, seq=s),
        grid=(b, nb),
        in_specs=[
            pl.BlockSpec(memory_space=pltpu.SMEM),
            pl.BlockSpec((None, BLOCK, ATTN_W), lambda bb, j: (bb, j, P_Q // ATTN_W)),
            kv_spec(kcol, prev), kv_spec(kcol, cur), kv_spec(kcol, nxt),
            kv_spec(vcol, prev), kv_spec(vcol, cur), kv_spec(vcol, nxt),
        ],
        out_specs=pl.BlockSpec((None, BLOCK, ATTN_W), lambda bb, j: (bb, j, 0)),
        out_shape=jax.ShapeDtypeStruct((b, s, ATTN_W), BF16),
        compiler_params=_cparams(("parallel", "arbitrary")),
        name="attn",
    )(head_params, proj3, proj3, proj3, proj3, proj3, proj3, proj3)


HALO = 8
CONV_TB = 512
CONV_CB = 512


def _conv_kernel(cur_ref, prev_ref, next_ref, w_ref, b_ref, o_ref, ext):
    i = pl.program_id(1)
    last = pl.num_programs(1) - 1
    tb = cur_ref.shape[0]
    ext[0:HALO, :] = jnp.where(i == 0, 0.0, prev_ref[...])
    ext[HALO:HALO + tb, :] = cur_ref[...]
    ext[HALO + tb:, :] = jnp.where(i == last, 0.0, next_ref[...])
    half = D_CONV // 2
    acc = b_ref[...] + w_ref[0:1, :] * ext[HALO - half:HALO - half + tb, :]
    for k in range(1, D_CONV):
        acc = acc + w_ref[k:k + 1, :] * ext[HALO - half + k:HALO - half + k + tb, :]
    o_ref[...] = acc * _sigmoid(acc)


def _conv_silu(proj3, conv_w, conv_b):
    b, s, _ = proj3.shape
    tb, cb = CONV_TB, CONV_CB
    c0 = P_XBC // cb
    nrow8 = s // HALO
    return pl.pallas_call(
        _conv_kernel,
        grid=(b, s // tb, XBC_W // cb),
        in_specs=[
            pl.BlockSpec((None, tb, cb), lambda bb, i, c: (bb, i, c0 + c)),
            pl.BlockSpec((None, HALO, cb), lambda bb, i, c: (bb, jnp.maximum(i * (tb // HALO) - 1, 0), c0 + c)),
            pl.BlockSpec((None, HALO, cb),
                         lambda bb, i, c: (bb, jnp.minimum((i + 1) * (tb // HALO), nrow8 - 1), c0 + c)),
            pl.BlockSpec((D_CONV, cb), lambda bb, i, c: (0, c)),
            pl.BlockSpec((1, cb), lambda bb, i, c: (0, c)),
        ],
        out_specs=pl.BlockSpec((None, tb, cb), lambda bb, i, c: (bb, i, c)),
        out_shape=jax.ShapeDtypeStruct((b, s, XBC_W), F32),
        scratch_shapes=[pltpu.VMEM((tb + 2 * HALO, cb), F32)],
        compiler_params=_cparams(("parallel", "arbitrary", "arbitrary")),
        name="conv",
    )(proj3, proj3, proj3, conv_w, conv_b)


N_PAIRS = N_SSM_HEADS // 2
PAIRS_PER_GROUP = N_PAIRS // N_SSM_GROUPS
GROUP_W = D_INNER // N_SSM_GROUPS


def _softplus(x):
    return jnp.maximum(x, 0.0) + jnp.log1p(jnp.exp(-jnp.abs(x)))


def _ssd_kernel(*refs, reverse):
    if reverse:
        xbc_ref, dt_ref, alog_ref, dtb_ref, yf_ref, z_ref, dskip_ref, gn_ref, o_ref, state, ybuf = refs
    else:
        xbc_ref, dt_ref, alog_ref, dtb_ref, o_ref, state = refs
        ybuf = o_ref
    L = CHUNK

    @pl.when(pl.program_id(1) == 0)
    def _():
        state[...] = jnp.zeros_like(state)

    dt = _softplus(dt_ref[...] + dtb_ref[...])
    dta = dt * (-jnp.exp(alog_ref[...]))
    row = lax.broadcasted_iota(jnp.int32, (L, L), 0)
    col = lax.broadcasted_iota(jnp.int32, (L, L), 1)
    tri = (col >= row) if reverse else (col <= row)
    cum = jnp.dot(tri.astype(F32), dta, precision=lax.Precision.HIGHEST, preferred_element_type=F32)
    cum_t = cum.T
    end = 0 if reverse else L - 1
    ecum = jnp.exp(cum)
    left = lax.broadcasted_iota(jnp.int32, (L, LANES), 1) < SSM_HEAD_DIM
    off = N_SSM_HEADS if reverse else 0

    for g in range(N_SSM_GROUPS):
        bg = xbc_ref[:, D_INNER + g * D_STATE:D_INNER + (g + 1) * D_STATE]
        cg = xbc_ref[:, D_INNER + (N_SSM_GROUPS + g) * D_STATE:D_INNER + (N_SSM_GROUPS + g + 1) * D_STATE]
        cb = lax.dot_general(cg.astype(BF16), bg.astype(BF16), (((1,), (1,)), ((), ())),
                             preferred_element_type=F32)
        bt = bg.T
        for pp in range(PAIRS_PER_GROUP):
            hp = g * PAIRS_PER_GROUP + pp
            h1, h2 = 2 * hp + off, 2 * hp + 1 + off
            xp = xbc_ref[:, hp * LANES:(hp + 1) * LANES]
            dtp = jnp.where(left, dt[:, h1:h1 + 1], dt[:, h2:h2 + 1])
            xdt = (xp * dtp).astype(BF16)
            zero = jnp.zeros_like(xdt)
            xbd = jnp.concatenate([jnp.where(left, xdt, zero), jnp.where(left, zero, xdt)], axis=0)
            ms, cds, bds, decs = [], [], [], []
            for h in (h1, h2):
                seg = cum[:, h:h + 1] - cum_t[h:h + 1, :]
                lmat = jnp.exp(jnp.where(tri, seg, NEG_BIG))
                ms.append((cb * lmat).astype(BF16))
                cds.append((cg * ecum[:, h:h + 1]).astype(BF16))
                tot = cum_t[h:h + 1, end:end + 1]
                bds.append((bt * jnp.exp(tot - cum_t[h:h + 1, :])).astype(BF16))
                decs.append(jnp.exp(tot))
            sp = state[hp]
            spb = sp.astype(BF16)
            zs = jnp.zeros_like(spb)
            sbd = jnp.concatenate([jnp.where(left, spb, zs), jnp.where(left, zs, spb)], axis=0)
            y = jnp.dot(jnp.concatenate(ms, axis=1), xbd, preferred_element_type=F32)
            y = y + jnp.dot(jnp.concatenate(cds, axis=1), sbd, preferred_element_type=F32)
            ybuf[:, hp * LANES:(hp + 1) * LANES] = y
            snew = jnp.dot(jnp.concatenate(bds, axis=1), xbd, preferred_element_type=F32)
            state[hp] = sp * jnp.where(left, decs[0], decs[1]) + snew

    if reverse:
        y = ybuf[...] + yf_ref[...] + dskip_ref[...] * xbc_ref[:, 0:D_INNER]
        z = z_ref[...]
        gated = y * (z * _sigmoid(z))
        for g in range(N_SSM_GROUPS):
            sl = slice(g * GROUP_W, (g + 1) * GROUP_W)
            o_ref[:, sl] = _rms(gated[:, sl], gn_ref[:, sl]).astype(o_ref.dtype)


def _ssd(xbc_act, proj3, alog_all, dtb_all, reverse, extras=()):
    b, s, _ = xbc_act.shape
    nc = s // CHUNK
    cidx = (lambda c: nc - 1 - c) if reverse else (lambda c: c)
    row_spec = lambda w, colblk: pl.BlockSpec((None, CHUNK, w), lambda bb, c: (bb, cidx(c), colblk))
    par_spec = lambda w: pl.BlockSpec((1, w), lambda bb, c: (0, 0))
    in_specs = [row_spec(XBC_W, 0), row_spec(LANES, P_DT // LANES), par_spec(LANES), par_spec(LANES)]
    args = [xbc_act, proj3, alog_all, dtb_all]
    scratch = [pltpu.VMEM((N_PAIRS, D_STATE, LANES), F32)]
    if reverse:
        yf, dskip, gnorm = extras
        in_specs += [row_spec(D_INNER, 0), row_spec(D_INNER, P_Z // D_INNER), par_spec(D_INNER), par_spec(D_INNER)]
        args += [yf, proj3, dskip, gnorm]
        scratch.append(pltpu.VMEM((CHUNK, D_INNER), F32))
        out_dtype = BF16
    else:
        out_dtype = F32
    return pl.pallas_call(
        functools.partial(_ssd_kernel, reverse=reverse),
        grid=(b, nc),
        in_specs=in_specs,
        out_specs=row_spec(D_INNER, 0),
        out_shape=jax.ShapeDtypeStruct((b, s, D_INNER), out_dtype),
        scratch_shapes=scratch,
        compiler_params=_cparams(("parallel", "arbitrary")),
        name="ssd_bwd" if reverse else "ssd_fwd",
    )(*args)


def _merge_kernel(attn_ref, ssm_ref, ga_ref, gs_ref, x_ref, wa_ref, ws_ref, wo_ref, gf_ref, x1_ref, hn_ref):
    a = jnp.dot(attn_ref[...], wa_ref[...], preferred_element_type=F32)
    s = jnp.dot(ssm_ref[...], ws_ref[...], preferred_element_type=F32)
    merged = _sigmoid(ga_ref[...]) * a + _sigmoid(gs_ref[...]) * s
    x1 = x_ref[...] + jnp.dot(merged.astype(BF16), wo_ref[...], preferred_element_type=F32)
    x1_ref[...] = x1
    hn_ref[...] = _rms(x1, gf_ref[...]).astype(BF16)


def _resident(shape):
    return pl.BlockSpec(shape, lambda *_: (0,) * len(shape), pipeline_mode=pl.Buffered(1))


def _merge(attn2, ssm2, proj2, x2, wa, ws, wo, g_ffn, tm=256):
    t = x2.shape[0]
    rows = lambda w, colblk=0: pl.BlockSpec((tm, w), lambda i: (i, colblk))
    return pl.pallas_call(
        _merge_kernel,
        grid=(t // tm,),
        in_specs=[
            rows(ATTN_W), rows(D_INNER), rows(D_MODEL, P_GA // D_MODEL), rows(D_MODEL, P_GS // D_MODEL),
            rows(D_MODEL), _resident((ATTN_W, D_MODEL)), _resident((D_INNER, D_MODEL)),
            _resident((D_MODEL, D_MODEL)), _resident((1, D_MODEL)),
        ],
        out_specs=[rows(D_MODEL), rows(D_MODEL)],
        out_shape=[jax.ShapeDtypeStruct((t, D_MODEL), F32), jax.ShapeDtypeStruct((t, D_MODEL), BF16)],
        compiler_params=_cparams(("parallel",)),
        name="merge",
    )(attn2, ssm2, proj2, proj2, x2, wa, ws, wo, g_ffn)


def _extract_topk(s, k_top):
    r = s.shape[0]
    iota = lax.broadcasted_iota(jnp.int32, s.shape, 0).astype(F32)
    pos = jnp.full(s.shape, float(k_top), F32)
    vals = []
    for k in range(k_top):
        m = jnp.max(s, axis=0, keepdims=True)
        idx = jnp.min(jnp.where(s == m, iota, float(r)), axis=0, keepdims=True)
        hit = iota == idx
        pos = jnp.where(hit, float(k), pos)
        s = jnp.where(hit, -jnp.inf, s)
        vals.append(m)
    return vals, pos


def _extract_topk_distinct(s, k_top):
    s0 = s
    vals = []
    for k in range(k_top):
        m = jnp.max(s, axis=0, keepdims=True)
        s = jnp.where(s == m, -jnp.inf, s)
        vals.append(m)
    removed = jnp.sum(jnp.where(s == -jnp.inf, 1.0, 0.0), axis=0, keepdims=True)
    pos = jnp.zeros(s.shape, F32)
    for k in range(k_top):
        pos = pos + jnp.where(vals[k] > s0, 1.0, 0.0)
    return vals, pos, removed


def _any_lane(flags):
    acc = flags[0]
    for f in flags[1:]:
        acc = acc | f
    return jnp.max(jnp.where(acc, 1.0, 0.0)) > 0.0


def _stack_rows(rows, n_rows):
    n = rows[0].shape[1]
    iota = lax.broadcasted_iota(jnp.int32, (n_rows, n), 0)
    out = jnp.zeros((n_rows, n), F32)
    for k, row in enumerate(rows):
        out = jnp.where(iota == k, row, out)
    return out


CAND_PAIRS = [(k1, k2) for k1 in range(PEER_TOPK) for k2 in range(PEER_TOPK) if (k1 + 1) * (k2 + 1) <= PEER_TOPK]
CAND_ROWS = -(-len(CAND_PAIRS) // 8) * 8


def _route_kernel(hn_ref, wq_ref, sk_ref, cnt_ref, a_ref, pos_ref, b_ref, sc_scr, pos_scr, val_scr):
    k_top = PEER_TOPK
    tm = hn_ref.shape[0]
    nch = tm // LANES
    q = jnp.dot(hn_ref[...], wq_ref[...], preferred_element_type=F32).astype(BF16)
    for v in range(2 * PEER_HEADS):
        qv = q[:, v * HALF_KEY:(v + 1) * HALF_KEY]
        s = lax.dot_general(sk_ref[v], qv, (((1,), (1,)), ((), ())), preferred_element_type=F32)
        for ch in range(nch):
            sc_scr[v, ch] = s[:, ch * LANES:(ch + 1) * LANES]

    def stage1(i, carry):
        v, pair = i // (nch // 2), i % (nch // 2)
        chunks = (2 * pair, 2 * pair + 1)
        flags = []
        for ch in chunks:
            vals, pos, removed = _extract_topk_distinct(sc_scr[v, ch], k_top)
            pos_scr[v, ch] = pos
            val_scr[v, ch] = _stack_rows(vals, k_top)
            flags.append(removed != float(k_top))

        @pl.when(_any_lane(flags))
        def _():
            for ch in chunks:
                vals, pos = _extract_topk(sc_scr[v, ch], k_top)
                pos_scr[v, ch] = pos
                val_scr[v, ch] = _stack_rows(vals, k_top)

        return carry

    lax.fori_loop(0, 2 * PEER_HEADS * (nch // 2), stage1, 0)

    def candidates(h, ch):
        v1 = val_scr[2 * h, ch]
        v2 = val_scr[2 * h + 1, ch]
        iota_c = lax.broadcasted_iota(jnp.int32, (CAND_ROWS, LANES), 0)
        cand = jnp.full((CAND_ROWS, LANES), -jnp.inf, F32)
        for r, (k1, k2) in enumerate(CAND_PAIRS):
            cand = jnp.where(iota_c == r, v1[k1:k1 + 1, :] + v2[k2:k2 + 1, :], cand)
        return cand

    def emit(h, ch, best, cpos):
        cs = slice(ch * LANES, (ch + 1) * LANES)
        sel = jnp.where(cpos < float(k_top), 1.0, 0.0)
        zsum = jnp.zeros((1, LANES), F32)
        for k in range(k_top):
            zsum = zsum + jnp.exp(best[k] - best[0])
        cnt_rows = [jnp.zeros((1, LANES), F32) for _ in range(k_top)]
        for r, (k1, k2) in enumerate(CAND_PAIRS):
            cnt_rows[k1] = cnt_rows[k1] + sel[r:r + 1, :]
        pos1 = pos_scr[2 * h, ch]
        cnt = jnp.zeros((N_KEYS, LANES), F32)
        for k in range(k_top):
            cnt = jnp.where(pos1 == float(k), cnt_rows[k], cnt)
        cnt_ref[h, :, cs] = cnt
        a_ref[h, :, cs] = jnp.exp(sc_scr[2 * h, ch] - val_scr[2 * h, ch, 0:1, :])
        pos_ref[h, :, cs] = pos_scr[2 * h + 1, ch]
        b_ref[h, :, cs] = jnp.exp(sc_scr[2 * h + 1, ch] - val_scr[2 * h + 1, ch, 0:1, :]) / zsum

    def stage2(h, carry):
        flags = []
        for ch in range(nch):
            best, cpos, removed = _extract_topk_distinct(candidates(h, ch), k_top)
            emit(h, ch, best, cpos)
            flags.append(removed != float(k_top + CAND_ROWS - len(CAND_PAIRS)))

        @pl.when(_any_lane(flags))
        def _():
            for ch in range(nch):
                best, cpos = _extract_topk(candidates(h, ch), k_top)
                emit(h, ch, best, cpos)

        return carry

    lax.fori_loop(0, PEER_HEADS, stage2, 0)


def _route(hn2, wq, sk, tm=512):
    t = hn2.shape[0]
    nch = tm // LANES
    nv = 2 * PEER_HEADS
    out = jax.ShapeDtypeStruct((PEER_HEADS, N_KEYS, t), F32)
    ospec = pl.BlockSpec((PEER_HEADS, N_KEYS, tm), lambda i: (0, 0, i))
    return pl.pallas_call(
        _route_kernel,
        grid=(t // tm,),
        in_specs=[
            pl.BlockSpec((tm, D_MODEL), lambda i: (i, 0)),
            _resident((D_MODEL, nv * HALF_KEY)),
            _resident((nv, N_KEYS, HALF_KEY)),
        ],
        out_specs=[ospec] * 4,
        out_shape=[out] * 4,
        scratch_shapes=[pltpu.VMEM((nv, nch, N_KEYS, LANES), F32), pltpu.VMEM((nv, nch, N_KEYS, LANES), F32),
                        pltpu.VMEM((nv, nch, PEER_TOPK, LANES), F32)],
        compiler_params=_cparams(("parallel",)),
        name="route",
    )(hn2, wq, sk)


SQRT_HALF = math.sqrt(0.5)


SUBLANES = 8
PEER_TE = SUBLANES * N_KEYS
PEER_SUB = 64


BF16_ROWS = 16


def _packed_rows(row):
    one = jnp.broadcast_to(row, (BF16_ROWS, LANES)).astype(BF16)
    return jnp.concatenate([one] * (PEER_SUB // BF16_ROWS), axis=0)


def _peer_kernel(hn_ref, u_ref, vt_ref, cnt_ref, a_ref, pos_ref, b_ref, x1_ref, gf_ref, y_ref,
                 acc_ref, ht_ref, p_ref, pos_bf, b_bf):
    j = pl.program_id(1)
    te, tm = ht_ref.shape
    rows_per_tile = te // N_KEYS

    @pl.when(j == 0)
    def _():
        acc_ref[...] = jnp.zeros_like(acc_ref)
        pos_bf[...] = pos_ref[...].astype(BF16)
        b_bf[...] = b_ref[...].astype(BF16)

    ht_ref[...] = lax.dot_general(u_ref[...], hn_ref[...], (((1,), (1,)), ((), ())),
                                  preferred_element_type=F32)
    zero = jnp.zeros((PEER_SUB, LANES), BF16)
    for c in range(tm // LANES):
        cs = slice(c * LANES, (c + 1) * LANES)
        for r in range(rows_per_tile):
            cnt_rows = [_packed_rows(cnt_ref[h, j, r:r + 1, cs]) for h in range(PEER_HEADS)]
            a_rows = [_packed_rows(a_ref[h, j, r:r + 1, cs]) for h in range(PEER_HEADS)]
            for s0 in range(0, N_KEYS, PEER_SUB):
                w = None
                for h in range(PEER_HEADS):
                    hk = slice(h * N_KEYS + s0, h * N_KEYS + s0 + PEER_SUB)
                    term = jnp.where(pos_bf[hk, cs] < cnt_rows[h], b_bf[hk, cs] * a_rows[h], zero)
                    w = term if w is None else w + term
                rs = slice(r * N_KEYS + s0, r * N_KEYS + s0 + PEER_SUB)
                hr = ht_ref[rs, cs]
                act = 0.5 * hr * (1.0 + lax.erf(hr * SQRT_HALF))
                p_ref[rs, cs] = act.astype(BF16) * w
    acc_ref[...] += jnp.dot(vt_ref[...], p_ref[...], preferred_element_type=F32)

    @pl.when(j == pl.num_programs(1) - 1)
    def _():
        y_ref[...] = _rms(x1_ref[...] + acc_ref[...].T, gf_ref[...])


def _peer(hn2, u_bf, vt_bf, route, x1, g_final, tm=512):
    t = hn2.shape[0]
    te = PEER_TE
    n_tiles = N_EXPERTS // te
    rows_per_tile = te // N_KEYS
    cnt, a, pos, b = route
    cnt = cnt.reshape(PEER_HEADS, n_tiles, rows_per_tile, t)
    a = a.reshape(PEER_HEADS, n_tiles, rows_per_tile, t)
    once = dict(pipeline_mode=pl.Buffered(1))
    pos = pos.reshape(PEER_HEADS * N_KEYS, t)
    b = b.reshape(PEER_HEADS * N_KEYS, t)
    rspec = pl.BlockSpec((PEER_HEADS * N_KEYS, tm), lambda i, j: (0, i), **once)
    tspec = pl.BlockSpec((PEER_HEADS, n_tiles, rows_per_tile, tm), lambda i, j: (0, 0, 0, i), **once)
    return pl.pallas_call(
        _peer_kernel,
        grid=(t // tm, n_tiles),
        in_specs=[
            pl.BlockSpec((tm, D_MODEL), lambda i, j: (i, 0), **once),
            pl.BlockSpec((te, D_MODEL), lambda i, j: (j, 0)),
            pl.BlockSpec((D_MODEL, te), lambda i, j: (0, j)),
            tspec, tspec, rspec, rspec,
            pl.BlockSpec((tm, D_MODEL), lambda i, j: (i, 0), **once),
            pl.BlockSpec((1, D_MODEL), lambda i, j: (0, 0)),
        ],
        out_specs=pl.BlockSpec((tm, D_MODEL), lambda i, j: (i, 0)),
        out_shape=jax.ShapeDtypeStruct((t, D_MODEL), F32),
        scratch_shapes=[pltpu.VMEM((D_MODEL, tm), F32), pltpu.VMEM((te, tm), F32), pltpu.VMEM((te, tm), BF16),
                        pltpu.VMEM((PEER_HEADS * N_KEYS, tm), BF16), pltpu.VMEM((PEER_HEADS * N_KEYS, tm), BF16)],
        compiler_params=_cparams(("parallel", "arbitrary")),
        name="peer",
    )(hn2, u_bf, vt_bf, cnt, a, pos, b, x1, g_final)


def _pad_lanes(v):
    return jnp.pad(v, (0, LANES - v.shape[0])).reshape(1, LANES)


def _prep(g_mix, w_in, attn_sink, conv_w, conv_b, a_log_f, a_log_b, dt_bias_f, dt_bias_b, d_skip, g_ssm_norm,
          w_attn_o, w_ssm_o, w_out, g_ffn, w_query, sub_keys, expert_u, expert_v, g_final):
    w = w_in[0]
    w_packed = jnp.concatenate(
        [w[:, _V_END:_Z_END], w[:, _DT_END:_IN_W], w[:, _Z_END:_XBC_END], w[:, :_V_END], w[:, _XBC_END:_DT_END],
         jnp.zeros((D_MODEL, LANES - 2 * N_SSM_HEADS), w.dtype)], axis=1).astype(BF16)
    slopes = jnp.exp2(-8.0 * jnp.arange(1, N_Q_HEADS + 1, dtype=F32) / N_Q_HEADS)
    return dict(
        g_mix=g_mix.reshape(1, D_MODEL),
        w_packed=w_packed,
        head_params=jnp.stack([slopes, attn_sink[0].astype(F32)]),
        conv_w=conv_w[0], conv_b=conv_b.reshape(1, XBC_W),
        alog_all=_pad_lanes(jnp.concatenate([a_log_f[0], a_log_b[0]])),
        dtb_all=_pad_lanes(jnp.concatenate([dt_bias_f[0], dt_bias_b[0]])),
        dskip=jnp.repeat(d_skip[0], SSM_HEAD_DIM).reshape(1, D_INNER),
        gnorm=g_ssm_norm.reshape(1, D_INNER),
        wa=w_attn_o[0].astype(BF16), ws=w_ssm_o[0].astype(BF16), wo=w_out[0].astype(BF16),
        g_ffn=g_ffn.reshape(1, D_MODEL),
        wq=w_query[0].astype(BF16),
        sk=sub_keys[0].reshape(2 * PEER_HEADS, N_KEYS, HALF_KEY).astype(BF16),
        u_bf=expert_u[0].astype(BF16),
        vt_bf=expert_v[0].T.astype(BF16),
        g_final=g_final.reshape(1, D_MODEL),
    )


def _trunk(x, p):
    b, s, d = x.shape
    t = b * s
    x2 = x.reshape(t, d)
    proj2 = _inproj(x2, p["g_mix"], p["w_packed"])
    proj3 = proj2.reshape(b, s, P_W)
    attn = _attention(proj3, p["head_params"])
    xbc_act = _conv_silu(proj3, p["conv_w"], p["conv_b"])
    y_f = _ssd(xbc_act, proj3, p["alog_all"], p["dtb_all"], reverse=False)
    ssm = _ssd(xbc_act, proj3, p["alog_all"], p["dtb_all"], reverse=True, extras=(y_f, p["dskip"], p["gnorm"]))
    x1, hn = _merge(attn.reshape(t, ATTN_W), ssm.reshape(t, D_INNER), proj2, x2,
                    p["wa"], p["ws"], p["wo"], p["g_ffn"])
    route = _route(hn, p["wq"], p["sk"])
    y = _peer(hn, p["u_bf"], p["vt_bf"], route, x1, p["g_final"])
    return y.reshape(b, s, d)


def kernel(x_prompt, x_sample, g_mix, w_in, attn_sink, conv_w, conv_b, a_log_f, a_log_b, dt_bias_f, dt_bias_b,
           d_skip, g_ssm_norm, w_attn_o, w_ssm_o, w_out, g_ffn, w_query, sub_keys, expert_u, expert_v, g_final):
    p = _prep(g_mix, w_in, attn_sink, conv_w, conv_b, a_log_f, a_log_b, dt_bias_f, dt_bias_b, d_skip, g_ssm_norm,
              w_attn_o, w_ssm_o, w_out, g_ffn, w_query, sub_keys, expert_u, expert_v, g_final)
    return (_trunk(x_prompt, p), _trunk(x_sample, p))
```

```python
import functools
import math

import jax
import jax.numpy as jnp
from jax import lax
from jax.experimental import pallas as pl
from jax.experimental.pallas import tpu as pltpu

F32 = jnp.float32
BF16 = jnp.bfloat16

D_MODEL = 2048
N_Q_HEADS = 16
N_KV_HEADS = 4
HEAD_DIM = 64
Q_PER_KV = N_Q_HEADS // N_KV_HEADS
WINDOW = 128
BLOCK = 128
ATTN_W = N_Q_HEADS * HEAD_DIM
KV_W = N_KV_HEADS * HEAD_DIM
NEG_BIG = -1e30
D_INNER = D_MODEL
SSM_HEAD_DIM = 64
N_SSM_HEADS = D_INNER // SSM_HEAD_DIM
N_SSM_GROUPS = 4
D_STATE = 128
D_CONV = 5
CHUNK = 128
XBC_W = D_INNER + 2 * N_SSM_GROUPS * D_STATE
N_KEYS = 128
N_EXPERTS = N_KEYS * N_KEYS
PEER_HEADS = 8
PEER_TOPK = 16
HALF_KEY = 128
EPS = 1e-6

_Q_END = ATTN_W
_K_END = _Q_END + KV_W
_V_END = _K_END + KV_W
_Z_END = _V_END + D_INNER
_XBC_END = _Z_END + XBC_W
_DT_END = _XBC_END + 2 * N_SSM_HEADS
_IN_W = _DT_END + 2 * D_MODEL

LANES = 128
P_Z = 0
P_GA = P_Z + D_INNER
P_GS = P_GA + D_MODEL
P_XBC = P_GS + D_MODEL
P_Q = P_XBC + XBC_W
P_K = P_Q + ATTN_W
P_V = P_K + KV_W
P_DT = P_V + KV_W
P_W = P_DT + LANES

VMEM_LIMIT = 56 * 1024 * 1024


def _cparams(sem, flags=None):
    return pltpu.CompilerParams(dimension_semantics=sem, vmem_limit_bytes=VMEM_LIMIT, flags=flags)


def _sigmoid(x):
    return 1.0 / (1.0 + jnp.exp(-x))


def _rms(x, g):
    ms = jnp.mean(x * x, axis=-1, keepdims=True)
    return (x * lax.rsqrt(ms + EPS)) * g


def _inproj_kernel(x_ref, g_ref, w_ref, o_ref, h_scr):
    @pl.when(pl.program_id(1) == 0)
    def _():
        h_scr[...] = _rms(x_ref[...], g_ref[...]).astype(BF16)

    o_ref[...] = jnp.dot(h_scr[...], w_ref[...], preferred_element_type=F32)


def _inproj(x2, g_mix, w_packed, tm=512, tn=2176):
    t = x2.shape[0]
    return pl.pallas_call(
        _inproj_kernel,
        grid=(t // tm, P_W // tn),
        in_specs=[
            pl.BlockSpec((tm, D_MODEL), lambda i, j: (i, 0)),
            pl.BlockSpec((1, D_MODEL), lambda i, j: (0, 0)),
            pl.BlockSpec((D_MODEL, tn), lambda i, j: (0, j)),
        ],
        out_specs=pl.BlockSpec((tm, tn), lambda i, j: (i, j)),
        out_shape=jax.ShapeDtypeStruct((t, P_W), F32),
        scratch_shapes=[pltpu.VMEM((tm, D_MODEL), BF16)],
        compiler_params=_cparams(("parallel", "arbitrary")),
        name="inproj",
    )(x2, g_mix, w_packed)


ATTN_ROWS = 32


def _attn_kernel(hp_ref, q_ref, kp_ref, kc_ref, kn_ref, vp_ref, vc_ref, vn_ref, o_ref, s_scr, p_scr, *, seq):
    j = pl.program_id(1)
    q = (q_ref[...] * (HEAD_DIM ** -0.5)).astype(BF16)
    kw = jnp.concatenate([kp_ref[...], kc_ref[...], kn_ref[...]], axis=0).astype(BF16)
    vw = jnp.concatenate([vp_ref[...], vc_ref[...], vn_ref[...]], axis=0).astype(BF16)
    qi = lax.broadcasted_iota(jnp.int32, (BLOCK, 3 * BLOCK), 0)
    km = lax.broadcasted_iota(jnp.int32, (BLOCK, 3 * BLOCK), 1)
    rel = qi - km + BLOCK
    key_pos = (j - 1) * BLOCK + km
    valid = (jnp.abs(rel) <= WINDOW) & (key_pos >= 0) & (key_pos < seq)
    dmask = jnp.where(valid, jnp.abs(rel).astype(F32), jnp.inf)

    for kv in range(N_KV_HEADS):
        qg = jnp.concatenate([q[:, (kv * Q_PER_KV + g) * HEAD_DIM:(kv * Q_PER_KV + g + 1) * HEAD_DIM]
                              for g in range(Q_PER_KV)], axis=0)
        kh = kw[:, kv * HEAD_DIM:(kv + 1) * HEAD_DIM]
        s_scr[kv] = lax.dot_general(qg, kh, (((1,), (1,)), ((), ())), preferred_element_type=F32)

    for h in range(N_Q_HEADS):
        kv, g = divmod(h, Q_PER_KV)
        sink = hp_ref[1, h]
        bias = hp_ref[0, h] * dmask
        for r0 in range(0, BLOCK, ATTN_ROWS):
            rows = slice(g * BLOCK + r0, g * BLOCK + r0 + ATTN_ROWS)
            s = s_scr[kv, rows, :] - bias[r0:r0 + ATTN_ROWS, :]
            mx = jnp.maximum(jnp.max(s, axis=-1, keepdims=True), sink)
            p = jnp.exp(s - mx)
            denom = jnp.sum(p, axis=-1, keepdims=True) + jnp.exp(sink - mx)
            p_scr[kv, rows, :] = (p / denom).astype(BF16)

    outs = []
    for kv in range(N_KV_HEADS):
        o = jnp.dot(p_scr[kv], vw[:, kv * HEAD_DIM:(kv + 1) * HEAD_DIM], preferred_element_type=F32)
        outs += [o[g * BLOCK:(g + 1) * BLOCK, :] for g in range(Q_PER_KV)]
    o_ref[...] = jnp.concatenate(outs, axis=-1).astype(o_ref.dtype)


def _attention(proj3, head_params):
    b, s, _ = proj3.shape
    nb = s // BLOCK
    kcol, vcol = P_K // KV_W, P_V // KV_W
    prev = lambda bb, j: jnp.maximum(j - 1, 0)
    nxt = lambda bb, j: jnp.minimum(j + 1, nb - 1)
    kv_spec = lambda col, f: pl.BlockSpec((None, BLOCK, KV_W), lambda bb, j: (bb, f(bb, j), col))
    cur = lambda bb, j: j
    return pl.pallas_call(
        functools.partial(_attn_kernel, seq=s),
        grid=(b, nb),
        in_specs=[
            pl.BlockSpec(memory_space=pltpu.SMEM),
            pl.BlockSpec((None, BLOCK, ATTN_W), lambda bb, j: (bb, j, P_Q // ATTN_W)),
            kv_spec(kcol, prev), kv_spec(kcol, cur), kv_spec(kcol, nxt),
            kv_spec(vcol, prev), kv_spec(vcol, cur), kv_spec(vcol, nxt),
        ],
        out_specs=pl.BlockSpec((None, BLOCK, ATTN_W), lambda bb, j: (bb, j, 0)),
        out_shape=jax.ShapeDtypeStruct((b, s, ATTN_W), BF16),
        scratch_shapes=[pltpu.VMEM((N_KV_HEADS, Q_PER_KV * BLOCK, 3 * BLOCK), F32),
                        pltpu.VMEM((N_KV_HEADS, Q_PER_KV * BLOCK, 3 * BLOCK), BF16)],
        compiler_params=_cparams(("parallel", "arbitrary")),
        name="attn",
    )(head_params, proj3, proj3, proj3, proj3, proj3, proj3, proj3)


HALO = 8
CONV_TB = 512
CONV_CB = 512


def _conv_kernel(cur_ref, prev_ref, next_ref, w_ref, b_ref, o_ref, ext):
    i = pl.program_id(1)
    last = pl.num_programs(1) - 1
    tb = cur_ref.shape[0]
    ext[0:HALO, :] = jnp.where(i == 0, 0.0, prev_ref[...])
    ext[HALO:HALO + tb, :] = cur_ref[...]
    ext[HALO + tb:, :] = jnp.where(i == last, 0.0, next_ref[...])
    half = D_CONV // 2
    acc = b_ref[...] + w_ref[0:1, :] * ext[HALO - half:HALO - half + tb, :]
    for k in range(1, D_CONV):
        acc = acc + w_ref[k:k + 1, :] * ext[HALO - half + k:HALO - half + k + tb, :]
    o_ref[...] = acc * _sigmoid(acc)


def _conv_silu(proj3, conv_w, conv_b):
    b, s, _ = proj3.shape
    tb, cb = CONV_TB, CONV_CB
    c0 = P_XBC // cb
    nrow8 = s // HALO
    return pl.pallas_call(
        _conv_kernel,
        grid=(b, s // tb, XBC_W // cb),
        in_specs=[
            pl.BlockSpec((None, tb, cb), lambda bb, i, c: (bb, i, c0 + c)),
            pl.BlockSpec((None, HALO, cb), lambda bb, i, c: (bb, jnp.maximum(i * (tb // HALO) - 1, 0), c0 + c)),
            pl.BlockSpec((None, HALO, cb),
                         lambda bb, i, c: (bb, jnp.minimum((i + 1) * (tb // HALO), nrow8 - 1), c0 + c)),
            pl.BlockSpec((D_CONV, cb), lambda bb, i, c: (0, c)),
            pl.BlockSpec((1, cb), lambda bb, i, c: (0, c)),
        ],
        out_specs=pl.BlockSpec((None, tb, cb), lambda bb, i, c: (bb, i, c)),
        out_shape=jax.ShapeDtypeStruct((b, s, XBC_W), F32),
        scratch_shapes=[pltpu.VMEM((tb + 2 * HALO, cb), F32)],
        compiler_params=_cparams(("parallel", "arbitrary", "arbitrary")),
        name="conv",
    )(proj3, proj3, proj3, conv_w, conv_b)


N_PAIRS = N_SSM_HEADS // 2
PAIRS_PER_GROUP = N_PAIRS // N_SSM_GROUPS
GROUP_W = D_INNER // N_SSM_GROUPS


def _softplus(x):
    return jnp.maximum(x, 0.0) + jnp.log1p(jnp.exp(-jnp.abs(x)))


def _ssd_kernel(*refs, reverse):
    if reverse:
        xbc_ref, dt_ref, alog_ref, dtb_ref, yf_ref, z_ref, dskip_ref, gn_ref, o_ref, state, ybuf = refs
    else:
        xbc_ref, dt_ref, alog_ref, dtb_ref, o_ref, state = refs
        ybuf = o_ref
    L = CHUNK

    @pl.when(pl.program_id(1) == 0)
    def _():
        state[...] = jnp.zeros_like(state)

    dt = _softplus(dt_ref[...] + dtb_ref[...])
    dta = dt * (-jnp.exp(alog_ref[...]))
    row = lax.broadcasted_iota(jnp.int32, (L, L), 0)
    col = lax.broadcasted_iota(jnp.int32, (L, L), 1)
    tri = (col >= row) if reverse else (col <= row)
    cum = jnp.dot(tri.astype(F32), dta, precision=lax.Precision.HIGHEST, preferred_element_type=F32)
    cum_t = cum.T
    end = 0 if reverse else L - 1
    ecum = jnp.exp(cum)
    left = lax.broadcasted_iota(jnp.int32, (L, LANES), 1) < SSM_HEAD_DIM
    off = N_SSM_HEADS if reverse else 0

    for g in range(N_SSM_GROUPS):
        bg = xbc_ref[:, D_INNER + g * D_STATE:D_INNER + (g + 1) * D_STATE]
        cg = xbc_ref[:, D_INNER + (N_SSM_GROUPS + g) * D_STATE:D_INNER + (N_SSM_GROUPS + g + 1) * D_STATE]
        cb = lax.dot_general(cg.astype(BF16), bg.astype(BF16), (((1,), (1,)), ((), ())),
                             preferred_element_type=F32)
        bt = bg.T
        for pp in range(PAIRS_PER_GROUP):
            hp = g * PAIRS_PER_GROUP + pp
            h1, h2 = 2 * hp + off, 2 * hp + 1 + off
            xp = xbc_ref[:, hp * LANES:(hp + 1) * LANES]
            dtp = jnp.where(left, dt[:, h1:h1 + 1], dt[:, h2:h2 + 1])
            xdt = (xp * dtp).astype(BF16)
            zero = jnp.zeros_like(xdt)
            xbd = jnp.concatenate([jnp.where(left, xdt, zero), jnp.where(left, zero, xdt)], axis=0)
            ms, cds, bds, decs = [], [], [], []
            for h in (h1, h2):
                seg = cum[:, h:h + 1] - cum_t[h:h + 1, :]
                lmat = jnp.exp(jnp.where(tri, seg, NEG_BIG))
                ms.append((cb * lmat).astype(BF16))
                cds.append((cg * ecum[:, h:h + 1]).astype(BF16))
                tot = cum_t[h:h + 1, end:end + 1]
                bds.append((bt * jnp.exp(tot - cum_t[h:h + 1, :])).astype(BF16))
                decs.append(jnp.exp(tot))
            sp = state[hp]
            spb = sp.astype(BF16)
            zs = jnp.zeros_like(spb)
            sbd = jnp.concatenate([jnp.where(left, spb, zs), jnp.where(left, zs, spb)], axis=0)
            y = jnp.dot(jnp.concatenate(ms, axis=1), xbd, preferred_element_type=F32)
            y = y + jnp.dot(jnp.concatenate(cds, axis=1), sbd, preferred_element_type=F32)
            ybuf[:, hp * LANES:(hp + 1) * LANES] = y
            snew = jnp.dot(jnp.concatenate(bds, axis=1), xbd, preferred_element_type=F32)
            state[hp] = sp * jnp.where(left, decs[0], decs[1]) + snew

    if reverse:
        y = ybuf[...] + yf_ref[...] + dskip_ref[...] * xbc_ref[:, 0:D_INNER]
        z = z_ref[...]
        gated = y * (z * _sigmoid(z))
        for g in range(N_SSM_GROUPS):
            sl = slice(g * GROUP_W, (g + 1) * GROUP_W)
            o_ref[:, sl] = _rms(gated[:, sl], gn_ref[:, sl]).astype(o_ref.dtype)


def _ssd(xbc_act, proj3, alog_all, dtb_all, reverse, extras=()):
    b, s, _ = xbc_act.shape
    nc = s // CHUNK
    cidx = (lambda c: nc - 1 - c) if reverse else (lambda c: c)
    row_spec = lambda w, colblk: pl.BlockSpec((None, CHUNK, w), lambda bb, c: (bb, cidx(c), colblk))
    par_spec = lambda w: pl.BlockSpec((1, w), lambda bb, c: (0, 0))
    in_specs = [row_spec(XBC_W, 0), row_spec(LANES, P_DT // LANES), par_spec(LANES), par_spec(LANES)]
    args = [xbc_act, proj3, alog_all, dtb_all]
    scratch = [pltpu.VMEM((N_PAIRS, D_STATE, LANES), F32)]
    if reverse:
        yf, dskip, gnorm = extras
        in_specs += [row_spec(D_INNER, 0), row_spec(D_INNER, P_Z // D_INNER), par_spec(D_INNER), par_spec(D_INNER)]
        args += [yf, proj3, dskip, gnorm]
        scratch.append(pltpu.VMEM((CHUNK, D_INNER), F32))
        out_dtype = BF16
    else:
        out_dtype = F32
    return pl.pallas_call(
        functools.partial(_ssd_kernel, reverse=reverse),
        grid=(b, nc),
        in_specs=in_specs,
        out_specs=row_spec(D_INNER, 0),
        out_shape=jax.ShapeDtypeStruct((b, s, D_INNER), out_dtype),
        scratch_shapes=scratch,
        compiler_params=_cparams(("parallel", "arbitrary")),
        name="ssd_bwd" if reverse else "ssd_fwd",
    )(*args)


def _merge_kernel(attn_ref, ssm_ref, ga_ref, gs_ref, x_ref, wa_ref, ws_ref, wo_ref, gf_ref, x1_ref, hn_ref):
    a = jnp.dot(attn_ref[...], wa_ref[...], preferred_element_type=F32)
    s = jnp.dot(ssm_ref[...], ws_ref[...], preferred_element_type=F32)
    merged = _sigmoid(ga_ref[...]) * a + _sigmoid(gs_ref[...]) * s
    x1 = x_ref[...] + jnp.dot(merged.astype(BF16), wo_ref[...], preferred_element_type=F32)
    x1_ref[...] = x1
    hn_ref[...] = _rms(x1, gf_ref[...]).astype(BF16)


def _resident(shape):
    return pl.BlockSpec(shape, lambda *_: (0,) * len(shape), pipeline_mode=pl.Buffered(1))


def _merge(attn2, ssm2, proj2, x2, wa, ws, wo, g_ffn, tm=256):
    t = x2.shape[0]
    rows = lambda w, colblk=0: pl.BlockSpec((tm, w), lambda i: (i, colblk))
    return pl.pallas_call(
        _merge_kernel,
        grid=(t // tm,),
        in_specs=[
            rows(ATTN_W), rows(D_INNER), rows(D_MODEL, P_GA // D_MODEL), rows(D_MODEL, P_GS // D_MODEL),
            rows(D_MODEL), _resident((ATTN_W, D_MODEL)), _resident((D_INNER, D_MODEL)),
            _resident((D_MODEL, D_MODEL)), _resident((1, D_MODEL)),
        ],
        out_specs=[rows(D_MODEL), rows(D_MODEL)],
        out_shape=[jax.ShapeDtypeStruct((t, D_MODEL), F32), jax.ShapeDtypeStruct((t, D_MODEL), BF16)],
        compiler_params=_cparams(("parallel",)),
        name="merge",
    )(attn2, ssm2, proj2, proj2, x2, wa, ws, wo, g_ffn)


def _extract_topk(s, k_top):
    r = s.shape[0]
    iota = lax.broadcasted_iota(jnp.int32, s.shape, 0).astype(F32)
    pos = jnp.full(s.shape, float(k_top), F32)
    vals = []
    for k in range(k_top):
        m = jnp.max(s, axis=0, keepdims=True)
        idx = jnp.min(jnp.where(s == m, iota, float(r)), axis=0, keepdims=True)
        hit = iota == idx
        pos = jnp.where(hit, float(k), pos)
        s = jnp.where(hit, -jnp.inf, s)
        vals.append(m)
    return vals, pos


def _extract_topk_distinct(s, k_top):
    s0 = s
    vals = []
    for k in range(k_top):
        m = jnp.max(s, axis=0, keepdims=True)
        s = jnp.where(s == m, -jnp.inf, s)
        vals.append(m)
    removed = jnp.sum(jnp.where(s == -jnp.inf, 1.0, 0.0), axis=0, keepdims=True)
    pos = jnp.zeros(s.shape, F32)
    for k in range(k_top):
        pos = pos + jnp.where(vals[k] > s0, 1.0, 0.0)
    return vals, pos, removed


def _any_lane(flags):
    acc = flags[0]
    for f in flags[1:]:
        acc = acc | f
    return jnp.max(jnp.where(acc, 1.0, 0.0)) > 0.0


def _stack_rows(rows, n_rows):
    n = rows[0].shape[1]
    iota = lax.broadcasted_iota(jnp.int32, (n_rows, n), 0)
    out = jnp.zeros((n_rows, n), F32)
    for k, row in enumerate(rows):
        out = jnp.where(iota == k, row, out)
    return out


CAND_PAIRS = [(k1, k2) for k1 in range(PEER_TOPK) for k2 in range(PEER_TOPK) if (k1 + 1) * (k2 + 1) <= PEER_TOPK]
CAND_ROWS = -(-len(CAND_PAIRS) // 8) * 8
ROUTE_ILP = 4


def _route_kernel(hn_ref, wq_ref, sk_ref, cnt_ref, a_ref, pos_ref, b_ref, sc_scr, pos_scr, val_scr):
    k_top = PEER_TOPK
    tm = hn_ref.shape[0]
    nch = tm // LANES
    q = jnp.dot(hn_ref[...], wq_ref[...], preferred_element_type=F32).astype(BF16)
    for v in range(2 * PEER_HEADS):
        qv = q[:, v * HALF_KEY:(v + 1) * HALF_KEY]
        s = lax.dot_general(sk_ref[v], qv, (((1,), (1,)), ((), ())), preferred_element_type=F32)
        for ch in range(nch):
            sc_scr[v, ch] = s[:, ch * LANES:(ch + 1) * LANES]

    def stage1(i, carry):
        v, grp = i // (nch // ROUTE_ILP), i % (nch // ROUTE_ILP)
        chunks = tuple(ROUTE_ILP * grp + d for d in range(ROUTE_ILP))
        flags = []
        for ch in chunks:
            vals, pos, removed = _extract_topk_distinct(sc_scr[v, ch], k_top)
            pos_scr[v, ch] = pos
            val_scr[v, ch] = _stack_rows(vals, k_top)
            flags.append(removed != float(k_top))

        @pl.when(_any_lane(flags))
        def _():
            for ch in chunks:
                vals, pos = _extract_topk(sc_scr[v, ch], k_top)
                pos_scr[v, ch] = pos
                val_scr[v, ch] = _stack_rows(vals, k_top)

        return carry

    lax.fori_loop(0, 2 * PEER_HEADS * (nch // ROUTE_ILP), stage1, 0)

    def candidates(h, ch):
        v1 = val_scr[2 * h, ch]
        v2 = val_scr[2 * h + 1, ch]
        iota_c = lax.broadcasted_iota(jnp.int32, (CAND_ROWS, LANES), 0)
        cand = jnp.full((CAND_ROWS, LANES), -jnp.inf, F32)
        for r, (k1, k2) in enumerate(CAND_PAIRS):
            cand = jnp.where(iota_c == r, v1[k1:k1 + 1, :] + v2[k2:k2 + 1, :], cand)
        return cand

    def emit(h, ch, best, cpos):
        cs = slice(ch * LANES, (ch + 1) * LANES)
        sel = jnp.where(cpos < float(k_top), 1.0, 0.0)
        zsum = jnp.zeros((1, LANES), F32)
        for k in range(k_top):
            zsum = zsum + jnp.exp(best[k] - best[0])
        cnt_rows = [jnp.zeros((1, LANES), F32) for _ in range(k_top)]
        for r, (k1, k2) in enumerate(CAND_PAIRS):
            cnt_rows[k1] = cnt_rows[k1] + sel[r:r + 1, :]
        pos1 = pos_scr[2 * h, ch]
        cnt = jnp.zeros((N_KEYS, LANES), F32)
        for k in range(k_top):
            cnt = jnp.where(pos1 == float(k), cnt_rows[k], cnt)
        cnt_ref[h, :, cs] = cnt
        a_ref[h, :, cs] = jnp.exp(sc_scr[2 * h, ch] - val_scr[2 * h, ch, 0:1, :])
        pos_ref[h, :, cs] = pos_scr[2 * h + 1, ch]
        b_ref[h, :, cs] = jnp.exp(sc_scr[2 * h + 1, ch] - val_scr[2 * h + 1, ch, 0:1, :]) / zsum

    def stage2(h, carry):
        flags = []
        for ch in range(nch):
            best, cpos, removed = _extract_topk_distinct(candidates(h, ch), k_top)
            emit(h, ch, best, cpos)
            flags.append(removed != float(k_top + CAND_ROWS - len(CAND_PAIRS)))

        @pl.when(_any_lane(flags))
        def _():
            for ch in range(nch):
                best, cpos = _extract_topk(candidates(h, ch), k_top)
                emit(h, ch, best, cpos)

        return carry

    lax.fori_loop(0, PEER_HEADS, stage2, 0)


def _route(hn2, wq, sk, tm=512):
    t = hn2.shape[0]
    nch = tm // LANES
    nv = 2 * PEER_HEADS
    out = jax.ShapeDtypeStruct((PEER_HEADS, N_KEYS, t), F32)
    ospec = pl.BlockSpec((PEER_HEADS, N_KEYS, tm), lambda i: (0, 0, i))
    return pl.pallas_call(
        _route_kernel,
        grid=(t // tm,),
        in_specs=[
            pl.BlockSpec((tm, D_MODEL), lambda i: (i, 0)),
            _resident((D_MODEL, nv * HALF_KEY)),
            _resident((nv, N_KEYS, HALF_KEY)),
        ],
        out_specs=[ospec] * 4,
        out_shape=[out] * 4,
        scratch_shapes=[pltpu.VMEM((nv, nch, N_KEYS, LANES), F32), pltpu.VMEM((nv, nch, N_KEYS, LANES), F32),
                        pltpu.VMEM((nv, nch, PEER_TOPK, LANES), F32)],
        compiler_params=_cparams(("parallel",)),
        name="route",
    )(hn2, wq, sk)


SQRT_HALF = math.sqrt(0.5)


SUBLANES = 8
PEER_TE = SUBLANES * N_KEYS
PEER_SUB = 64


BF16_ROWS = 16


def _packed_rows(row):
    one = jnp.broadcast_to(row, (BF16_ROWS, LANES)).astype(BF16)
    return jnp.concatenate([one] * (PEER_SUB // BF16_ROWS), axis=0)


def _peer_kernel(hn_ref, u_ref, vt_ref, cnt_ref, a_ref, pos_ref, b_ref, x1_ref, gf_ref, y_ref,
                 acc_ref, ht_ref, p_ref, pos_bf, b_bf):
    j = pl.program_id(1)
    te, tm = ht_ref.shape
    rows_per_tile = te // N_KEYS

    @pl.when(j == 0)
    def _():
        acc_ref[...] = jnp.zeros_like(acc_ref)
        pos_bf[...] = pos_ref[...].astype(BF16)
        b_bf[...] = b_ref[...].astype(BF16)

    ht_ref[...] = lax.dot_general(u_ref[...], hn_ref[...], (((1,), (1,)), ((), ())),
                                  preferred_element_type=F32)
    zero = jnp.zeros((PEER_SUB, LANES), BF16)
    for c in range(tm // LANES):
        cs = slice(c * LANES, (c + 1) * LANES)
        for r in range(rows_per_tile):
            cnt_rows = [_packed_rows(cnt_ref[h, j, r:r + 1, cs]) for h in range(PEER_HEADS)]
            a_rows = [_packed_rows(a_ref[h, j, r:r + 1, cs]) for h in range(PEER_HEADS)]
            for s0 in range(0, N_KEYS, PEER_SUB):
                w = None
                for h in range(PEER_HEADS):
                    hk = slice(h * N_KEYS + s0, h * N_KEYS + s0 + PEER_SUB)
                    term = jnp.where(pos_bf[hk, cs] < cnt_rows[h], b_bf[hk, cs] * a_rows[h], zero)
                    w = term if w is None else w + term
                rs = slice(r * N_KEYS + s0, r * N_KEYS + s0 + PEER_SUB)
                hr = ht_ref[rs, cs]
                act = 0.5 * hr * (1.0 + lax.erf(hr * SQRT_HALF))
                p_ref[rs, cs] = act.astype(BF16) * w
    acc_ref[...] += jnp.dot(vt_ref[...], p_ref[...], preferred_element_type=F32)

    @pl.when(j == pl.num_programs(1) - 1)
    def _():
        y_ref[...] = _rms(x1_ref[...] + acc_ref[...].T, gf_ref[...])


def _peer(hn2, u_bf, vt_bf, route, x1, g_final, tm=512):
    t = hn2.shape[0]
    te = PEER_TE
    n_tiles = N_EXPERTS // te
    rows_per_tile = te // N_KEYS
    cnt, a, pos, b = route
    cnt = cnt.reshape(PEER_HEADS, n_tiles, rows_per_tile, t)
    a = a.reshape(PEER_HEADS, n_tiles, rows_per_tile, t)
    once = dict(pipeline_mode=pl.Buffered(1))
    pos = pos.reshape(PEER_HEADS * N_KEYS, t)
    b = b.reshape(PEER_HEADS * N_KEYS, t)
    rspec = pl.BlockSpec((PEER_HEADS * N_KEYS, tm), lambda i, j: (0, i), **once)
    tspec = pl.BlockSpec((PEER_HEADS, n_tiles, rows_per_tile, tm), lambda i, j: (0, 0, 0, i), **once)
    return pl.pallas_call(
        _peer_kernel,
        grid=(t // tm, n_tiles),
        in_specs=[
            pl.BlockSpec((tm, D_MODEL), lambda i, j: (i, 0), **once),
            pl.BlockSpec((te, D_MODEL), lambda i, j: (j, 0)),
            pl.BlockSpec((None, D_MODEL, te), lambda i, j: (j, 0, 0)),
            tspec, tspec, rspec, rspec,
            pl.BlockSpec((tm, D_MODEL), lambda i, j: (i, 0), **once),
            pl.BlockSpec((1, D_MODEL), lambda i, j: (0, 0)),
        ],
        out_specs=pl.BlockSpec((tm, D_MODEL), lambda i, j: (i, 0)),
        out_shape=jax.ShapeDtypeStruct((t, D_MODEL), F32),
        scratch_shapes=[pltpu.VMEM((D_MODEL, tm), F32), pltpu.VMEM((te, tm), F32), pltpu.VMEM((te, tm), BF16),
                        pltpu.VMEM((PEER_HEADS * N_KEYS, tm), BF16), pltpu.VMEM((PEER_HEADS * N_KEYS, tm), BF16)],
        compiler_params=_cparams(("parallel", "arbitrary")),
        name="peer",
    )(hn2, u_bf, vt_bf, cnt, a, pos, b, x1, g_final)


def _pad_lanes(v):
    return jnp.pad(v, (0, LANES - v.shape[0])).reshape(1, LANES)


def _prep(g_mix, w_in, attn_sink, conv_w, conv_b, a_log_f, a_log_b, dt_bias_f, dt_bias_b, d_skip, g_ssm_norm,
          w_attn_o, w_ssm_o, w_out, g_ffn, w_query, sub_keys, expert_u, expert_v, g_final):
    w = w_in[0].astype(BF16)
    w_packed = jnp.concatenate(
        [w[:, _V_END:_Z_END], w[:, _DT_END:_IN_W], w[:, _Z_END:_XBC_END], w[:, :_V_END], w[:, _XBC_END:_DT_END],
         jnp.zeros((D_MODEL, LANES - 2 * N_SSM_HEADS), w.dtype)], axis=1)
    slopes = jnp.exp2(-8.0 * jnp.arange(1, N_Q_HEADS + 1, dtype=F32) / N_Q_HEADS)
    return dict(
        g_mix=g_mix.reshape(1, D_MODEL),
        w_packed=w_packed,
        head_params=jnp.stack([slopes, attn_sink[0].astype(F32)]),
        conv_w=conv_w[0], conv_b=conv_b.reshape(1, XBC_W),
        alog_all=_pad_lanes(jnp.concatenate([a_log_f[0], a_log_b[0]])),
        dtb_all=_pad_lanes(jnp.concatenate([dt_bias_f[0], dt_bias_b[0]])),
        dskip=jnp.repeat(d_skip[0], SSM_HEAD_DIM).reshape(1, D_INNER),
        gnorm=g_ssm_norm.reshape(1, D_INNER),
        wa=w_attn_o[0].astype(BF16), ws=w_ssm_o[0].astype(BF16), wo=w_out[0].astype(BF16),
        g_ffn=g_ffn.reshape(1, D_MODEL),
        wq=w_query[0].astype(BF16),
        sk=sub_keys[0].reshape(2 * PEER_HEADS, N_KEYS, HALF_KEY).astype(BF16),
        u_bf=expert_u[0].astype(BF16),
        vt_bf=expert_v[0].astype(BF16).reshape(N_EXPERTS // PEER_TE, PEER_TE, D_MODEL).transpose(0, 2, 1),
        g_final=g_final.reshape(1, D_MODEL),
    )


def _trunk(x, p):
    b, s, d = x.shape
    t = b * s
    x2 = x.reshape(t, d)
    proj2 = _inproj(x2, p["g_mix"], p["w_packed"])
    proj3 = proj2.reshape(b, s, P_W)
    attn = _attention(proj3, p["head_params"])
    xbc_act = _conv_silu(proj3, p["conv_w"], p["conv_b"])
    y_f = _ssd(xbc_act, proj3, p["alog_all"], p["dtb_all"], reverse=False)
    ssm = _ssd(xbc_act, proj3, p["alog_all"], p["dtb_all"], reverse=True, extras=(y_f, p["dskip"], p["gnorm"]))
    x1, hn = _merge(attn.reshape(t, ATTN_W), ssm.reshape(t, D_INNER), proj2, x2,
                    p["wa"], p["ws"], p["wo"], p["g_ffn"])
    route = _route(hn, p["wq"], p["sk"])
    y = _peer(hn, p["u_bf"], p["vt_bf"], route, x1, p["g_final"])
    return y.reshape(b, s, d)


def kernel(x_prompt, x_sample, g_mix, w_in, attn_sink, conv_w, conv_b, a_log_f, a_log_b, dt_bias_f, dt_bias_b,
           d_skip, g_ssm_norm, w_attn_o, w_ssm_o, w_out, g_ffn, w_query, sub_keys, expert_u, expert_v, g_final):
    p = _prep(g_mix, w_in, attn_sink, conv_w, conv_b, a_log_f, a_log_b, dt_bias_f, dt_bias_b, d_skip, g_ssm_norm,
              w_attn_o, w_ssm_o, w_out, g_ffn, w_query, sub_keys, expert_u, expert_v, g_final)
    return (_trunk(x_prompt, p), _trunk(x_sample, p))
```

```python
import functools
import math

import jax
import jax.numpy as jnp
from jax import lax
from jax.experimental import pallas as pl
from jax.experimental.pallas import tpu as pltpu

F32 = jnp.float32
BF16 = jnp.bfloat16

D_MODEL = 2048
N_Q_HEADS = 16
N_KV_HEADS = 4
HEAD_DIM = 64
Q_PER_KV = N_Q_HEADS // N_KV_HEADS
WINDOW = 128
BLOCK = 128
ATTN_W = N_Q_HEADS * HEAD_DIM
KV_W = N_KV_HEADS * HEAD_DIM
NEG_BIG = -1e30
D_INNER = D_MODEL
SSM_HEAD_DIM = 64
N_SSM_HEADS = D_INNER // SSM_HEAD_DIM
N_SSM_GROUPS = 4
D_STATE = 128
D_CONV = 5
CHUNK = 128
XBC_W = D_INNER + 2 * N_SSM_GROUPS * D_STATE
N_KEYS = 128
N_EXPERTS = N_KEYS * N_KEYS
PEER_HEADS = 8
PEER_TOPK = 16
HALF_KEY = 128
EPS = 1e-6

_Q_END = ATTN_W
_K_END = _Q_END + KV_W
_V_END = _K_END + KV_W
_Z_END = _V_END + D_INNER
_XBC_END = _Z_END + XBC_W
_DT_END = _XBC_END + 2 * N_SSM_HEADS
_IN_W = _DT_END + 2 * D_MODEL

LANES = 128
P_Z = 0
P_GA = P_Z + D_INNER
P_GS = P_GA + D_MODEL
P_XBC = P_GS + D_MODEL
P_Q = P_XBC + XBC_W
P_K = P_Q + ATTN_W
P_V = P_K + KV_W
P_DT = P_V + KV_W
P_W = P_DT + LANES

VMEM_LIMIT = 56 * 1024 * 1024


def _cparams(sem, flags=None):
    return pltpu.CompilerParams(dimension_semantics=sem, vmem_limit_bytes=VMEM_LIMIT, flags=flags)


def _sigmoid(x):
    return 1.0 / (1.0 + jnp.exp(-x))


def _rms(x, g):
    ms = jnp.mean(x * x, axis=-1, keepdims=True)
    return (x * lax.rsqrt(ms + EPS)) * g


def _inproj_kernel(x_ref, g_ref, w_ref, o_ref, h_scr):
    @pl.when(pl.program_id(1) == 0)
    def _():
        h_scr[...] = _rms(x_ref[...], g_ref[...]).astype(BF16)

    o_ref[...] = jnp.dot(h_scr[...], w_ref[...], preferred_element_type=F32)


def _inproj(x2, g_mix, w_packed, tm=512, tn=2176):
    t = x2.shape[0]
    return pl.pallas_call(
        _inproj_kernel,
        grid=(t // tm, P_W // tn),
        in_specs=[
            pl.BlockSpec((tm, D_MODEL), lambda i, j: (i, 0)),
            pl.BlockSpec((1, D_MODEL), lambda i, j: (0, 0)),
            pl.BlockSpec((D_MODEL, tn), lambda i, j: (0, j)),
        ],
        out_specs=pl.BlockSpec((tm, tn), lambda i, j: (i, j)),
        out_shape=jax.ShapeDtypeStruct((t, P_W), F32),
        scratch_shapes=[pltpu.VMEM((tm, D_MODEL), BF16)],
        compiler_params=_cparams(("parallel", "arbitrary")),
        name="inproj",
    )(x2, g_mix, w_packed)


ATTN_ROWS = 32


def _attn_kernel(hp_ref, q_ref, kp_ref, kc_ref, kn_ref, vp_ref, vc_ref, vn_ref, o_ref, s_scr, p_scr, *, seq):
    j = pl.program_id(1)
    q = (q_ref[...] * (HEAD_DIM ** -0.5)).astype(BF16)
    kw = jnp.concatenate([kp_ref[...], kc_ref[...], kn_ref[...]], axis=0).astype(BF16)
    vw = jnp.concatenate([vp_ref[...], vc_ref[...], vn_ref[...]], axis=0).astype(BF16)
    qi = lax.broadcasted_iota(jnp.int32, (BLOCK, 3 * BLOCK), 0)
    km = lax.broadcasted_iota(jnp.int32, (BLOCK, 3 * BLOCK), 1)
    rel = qi - km + BLOCK
    key_pos = (j - 1) * BLOCK + km
    valid = (jnp.abs(rel) <= WINDOW) & (key_pos >= 0) & (key_pos < seq)
    dmask = jnp.where(valid, jnp.abs(rel).astype(F32), jnp.inf)

    for kv in range(N_KV_HEADS):
        qg = jnp.concatenate([q[:, (kv * Q_PER_KV + g) * HEAD_DIM:(kv * Q_PER_KV + g + 1) * HEAD_DIM]
                              for g in range(Q_PER_KV)], axis=0)
        kh = kw[:, kv * HEAD_DIM:(kv + 1) * HEAD_DIM]
        s_scr[kv] = lax.dot_general(qg, kh, (((1,), (1,)), ((), ())), preferred_element_type=F32)

    for h in range(N_Q_HEADS):
        kv, g = divmod(h, Q_PER_KV)
        sink = hp_ref[1, h]
        bias = hp_ref[0, h] * dmask
        for r0 in range(0, BLOCK, ATTN_ROWS):
            rows = slice(g * BLOCK + r0, g * BLOCK + r0 + ATTN_ROWS)
            s = s_scr[kv, rows, :] - bias[r0:r0 + ATTN_ROWS, :]
            mx = jnp.maximum(jnp.max(s, axis=-1, keepdims=True), sink)
            p = jnp.exp(s - mx)
            denom = jnp.sum(p, axis=-1, keepdims=True) + jnp.exp(sink - mx)
            p_scr[kv, rows, :] = (p / denom).astype(BF16)

    outs = []
    for kv in range(N_KV_HEADS):
        o = jnp.dot(p_scr[kv], vw[:, kv * HEAD_DIM:(kv + 1) * HEAD_DIM], preferred_element_type=F32)
        outs += [o[g * BLOCK:(g + 1) * BLOCK, :] for g in range(Q_PER_KV)]
    o_ref[...] = jnp.concatenate(outs, axis=-1).astype(o_ref.dtype)


def _attention(proj3, head_params):
    b, s, _ = proj3.shape
    nb = s // BLOCK
    kcol, vcol = P_K // KV_W, P_V // KV_W
    prev = lambda bb, j: jnp.maximum(j - 1, 0)
    nxt = lambda bb, j: jnp.minimum(j + 1, nb - 1)
    kv_spec = lambda col, f: pl.BlockSpec((None, BLOCK, KV_W), lambda bb, j: (bb, f(bb, j), col))
    cur = lambda bb, j: j
    return pl.pallas_call(
        functools.partial(_attn_kernel, seq=s),
        grid=(b, nb),
        in_specs=[
            pl.BlockSpec(memory_space=pltpu.SMEM),
            pl.BlockSpec((None, BLOCK, ATTN_W), lambda bb, j: (bb, j, P_Q // ATTN_W)),
            kv_spec(kcol, prev), kv_spec(kcol, cur), kv_spec(kcol, nxt),
            kv_spec(vcol, prev), kv_spec(vcol, cur), kv_spec(vcol, nxt),
        ],
        out_specs=pl.BlockSpec((None, BLOCK, ATTN_W), lambda bb, j: (bb, j, 0)),
        out_shape=jax.ShapeDtypeStruct((b, s, ATTN_W), BF16),
        scratch_shapes=[pltpu.VMEM((N_KV_HEADS, Q_PER_KV * BLOCK, 3 * BLOCK), F32),
                        pltpu.VMEM((N_KV_HEADS, Q_PER_KV * BLOCK, 3 * BLOCK), BF16)],
        compiler_params=_cparams(("parallel", "arbitrary")),
        name="attn",
    )(head_params, proj3, proj3, proj3, proj3, proj3, proj3, proj3)


HALO = 8
CONV_TB = 512
CONV_CB = 512


def _conv_kernel(cur_ref, prev_ref, next_ref, w_ref, b_ref, o_ref, ext):
    i = pl.program_id(1)
    last = pl.num_programs(1) - 1
    tb = cur_ref.shape[0]
    ext[0:HALO, :] = jnp.where(i == 0, 0.0, prev_ref[...])
    ext[HALO:HALO + tb, :] = cur_ref[...]
    ext[HALO + tb:, :] = jnp.where(i == last, 0.0, next_ref[...])
    half = D_CONV // 2
    acc = b_ref[...] + w_ref[0:1, :] * ext[HALO - half:HALO - half + tb, :]
    for k in range(1, D_CONV):
        acc = acc + w_ref[k:k + 1, :] * ext[HALO - half + k:HALO - half + k + tb, :]
    o_ref[...] = acc * _sigmoid(acc)


def _conv_silu(proj3, conv_w, conv_b):
    b, s, _ = proj3.shape
    tb, cb = CONV_TB, CONV_CB
    c0 = P_XBC // cb
    nrow8 = s // HALO
    return pl.pallas_call(
        _conv_kernel,
        grid=(b, s // tb, XBC_W // cb),
        in_specs=[
            pl.BlockSpec((None, tb, cb), lambda bb, i, c: (bb, i, c0 + c)),
            pl.BlockSpec((None, HALO, cb), lambda bb, i, c: (bb, jnp.maximum(i * (tb // HALO) - 1, 0), c0 + c)),
            pl.BlockSpec((None, HALO, cb),
                         lambda bb, i, c: (bb, jnp.minimum((i + 1) * (tb // HALO), nrow8 - 1), c0 + c)),
            pl.BlockSpec((D_CONV, cb), lambda bb, i, c: (0, c)),
            pl.BlockSpec((1, cb), lambda bb, i, c: (0, c)),
        ],
        out_specs=pl.BlockSpec((None, tb, cb), lambda bb, i, c: (bb, i, c)),
        out_shape=jax.ShapeDtypeStruct((b, s, XBC_W), F32),
        scratch_shapes=[pltpu.VMEM((tb + 2 * HALO, cb), F32)],
        compiler_params=_cparams(("parallel", "arbitrary", "arbitrary")),
        name="conv",
    )(proj3, proj3, proj3, conv_w, conv_b)


N_PAIRS = N_SSM_HEADS // 2
PAIRS_PER_GROUP = N_PAIRS // N_SSM_GROUPS
GROUP_W = D_INNER // N_SSM_GROUPS


def _softplus(x):
    return jnp.maximum(x, 0.0) + jnp.log1p(jnp.exp(-jnp.abs(x)))


def _ssd_kernel(*refs, reverse):
    if reverse:
        xbc_ref, dt_ref, alog_ref, dtb_ref, yf_ref, z_ref, dskip_ref, gn_ref, o_ref, state, ybuf = refs
    else:
        xbc_ref, dt_ref, alog_ref, dtb_ref, o_ref, state = refs
        ybuf = o_ref
    L = CHUNK

    @pl.when(pl.program_id(1) == 0)
    def _():
        state[...] = jnp.zeros_like(state)

    dt = _softplus(dt_ref[...] + dtb_ref[...])
    dta = dt * (-jnp.exp(alog_ref[...]))
    row = lax.broadcasted_iota(jnp.int32, (L, L), 0)
    col = lax.broadcasted_iota(jnp.int32, (L, L), 1)
    tri = (col >= row) if reverse else (col <= row)
    cum = jnp.dot(tri.astype(F32), dta, precision=lax.Precision.HIGHEST, preferred_element_type=F32)
    cum_t = cum.T
    end = 0 if reverse else L - 1
    ecum = jnp.exp(cum)
    left = lax.broadcasted_iota(jnp.int32, (L, LANES), 1) < SSM_HEAD_DIM
    off = N_SSM_HEADS if reverse else 0

    for g in range(N_SSM_GROUPS):
        bg = xbc_ref[:, D_INNER + g * D_STATE:D_INNER + (g + 1) * D_STATE]
        cg = xbc_ref[:, D_INNER + (N_SSM_GROUPS + g) * D_STATE:D_INNER + (N_SSM_GROUPS + g + 1) * D_STATE]
        cb = lax.dot_general(cg.astype(BF16), bg.astype(BF16), (((1,), (1,)), ((), ())),
                             preferred_element_type=F32)
        bt = bg.T
        for pp in range(PAIRS_PER_GROUP):
            hp = g * PAIRS_PER_GROUP + pp
            h1, h2 = 2 * hp + off, 2 * hp + 1 + off
            xp = xbc_ref[:, hp * LANES:(hp + 1) * LANES]
            dtp = jnp.where(left, dt[:, h1:h1 + 1], dt[:, h2:h2 + 1])
            xdt = (xp * dtp).astype(BF16)
            zero = jnp.zeros_like(xdt)
            xbd = jnp.concatenate([jnp.where(left, xdt, zero), jnp.where(left, zero, xdt)], axis=0)
            ms, cds, bds, decs = [], [], [], []
            for h in (h1, h2):
                seg = cum[:, h:h + 1] - cum_t[h:h + 1, :]
                lmat = jnp.exp(jnp.where(tri, seg, NEG_BIG))
                ms.append((cb * lmat).astype(BF16))
                cds.append((cg * ecum[:, h:h + 1]).astype(BF16))
                tot = cum_t[h:h + 1, end:end + 1]
                bds.append((bt * jnp.exp(tot - cum_t[h:h + 1, :])).astype(BF16))
                decs.append(jnp.exp(tot))
            sp = state[hp]
            spb = sp.astype(BF16)
            zs = jnp.zeros_like(spb)
            sbd = jnp.concatenate([jnp.where(left, spb, zs), jnp.where(left, zs, spb)], axis=0)
            y = jnp.dot(jnp.concatenate(ms, axis=1), xbd, preferred_element_type=F32)
            y = y + jnp.dot(jnp.concatenate(cds, axis=1), sbd, preferred_element_type=F32)
            ybuf[:, hp * LANES:(hp + 1) * LANES] = y
            snew = jnp.dot(jnp.concatenate(bds, axis=1), xbd, preferred_element_type=F32)
            state[hp] = sp * jnp.where(left, decs[0], decs[1]) + snew

    if reverse:
        y = ybuf[...] + yf_ref[...] + dskip_ref[...] * xbc_ref[:, 0:D_INNER]
        z = z_ref[...]
        gated = y * (z * _sigmoid(z))
        for g in range(N_SSM_GROUPS):
            sl = slice(g * GROUP_W, (g + 1) * GROUP_W)
            o_ref[:, sl] = _rms(gated[:, sl], gn_ref[:, sl]).astype(o_ref.dtype)


def _ssd(xbc_act, proj3, alog_all, dtb_all, reverse, extras=()):
    b, s, _ = xbc_act.shape
    nc = s // CHUNK
    cidx = (lambda c: nc - 1 - c) if reverse else (lambda c: c)
    row_spec = lambda w, colblk: pl.BlockSpec((None, CHUNK, w), lambda bb, c: (bb, cidx(c), colblk))
    par_spec = lambda w: pl.BlockSpec((1, w), lambda bb, c: (0, 0))
    in_specs = [row_spec(XBC_W, 0), row_spec(LANES, P_DT // LANES), par_spec(LANES), par_spec(LANES)]
    args = [xbc_act, proj3, alog_all, dtb_all]
    scratch = [pltpu.VMEM((N_PAIRS, D_STATE, LANES), F32)]
    if reverse:
        yf, dskip, gnorm = extras
        in_specs += [row_spec(D_INNER, 0), row_spec(D_INNER, P_Z // D_INNER), par_spec(D_INNER), par_spec(D_INNER)]
        args += [yf, proj3, dskip, gnorm]
        scratch.append(pltpu.VMEM((CHUNK, D_INNER), F32))
        out_dtype = BF16
    else:
        out_dtype = F32
    return pl.pallas_call(
        functools.partial(_ssd_kernel, reverse=reverse),
        grid=(b, nc),
        in_specs=in_specs,
        out_specs=row_spec(D_INNER, 0),
        out_shape=jax.ShapeDtypeStruct((b, s, D_INNER), out_dtype),
        scratch_shapes=scratch,
        compiler_params=_cparams(("parallel", "arbitrary")),
        name="ssd_bwd" if reverse else "ssd_fwd",
    )(*args)


def _merge_kernel(attn_ref, ssm_ref, ga_ref, gs_ref, x_ref, wa_ref, ws_ref, wo_ref, gf_ref,
                  x1_ref, hn_ref, hnt_ref):
    a = jnp.dot(attn_ref[...], wa_ref[...], preferred_element_type=F32)
    s = jnp.dot(ssm_ref[...], ws_ref[...], preferred_element_type=F32)
    merged = _sigmoid(ga_ref[...]) * a + _sigmoid(gs_ref[...]) * s
    x1 = x_ref[...] + jnp.dot(merged.astype(BF16), wo_ref[...], preferred_element_type=F32)
    x1_ref[...] = x1
    hn = _rms(x1, gf_ref[...])
    hn_ref[...] = hn.astype(BF16)
    hnt_ref[...] = hn.T.astype(BF16)


def _resident(shape):
    return pl.BlockSpec(shape, lambda *_: (0,) * len(shape), pipeline_mode=pl.Buffered(1))


def _merge(attn2, ssm2, proj2, x2, wa, ws, wo, g_ffn, tm=256):
    t = x2.shape[0]
    rows = lambda w, colblk=0: pl.BlockSpec((tm, w), lambda i: (i, colblk))
    return pl.pallas_call(
        _merge_kernel,
        grid=(t // tm,),
        in_specs=[
            rows(ATTN_W), rows(D_INNER), rows(D_MODEL, P_GA // D_MODEL), rows(D_MODEL, P_GS // D_MODEL),
            rows(D_MODEL), _resident((ATTN_W, D_MODEL)), _resident((D_INNER, D_MODEL)),
            _resident((D_MODEL, D_MODEL)), _resident((1, D_MODEL)),
        ],
        out_specs=[rows(D_MODEL), rows(D_MODEL), pl.BlockSpec((D_MODEL, tm), lambda i: (0, i))],
        out_shape=[jax.ShapeDtypeStruct((t, D_MODEL), F32), jax.ShapeDtypeStruct((t, D_MODEL), BF16),
                   jax.ShapeDtypeStruct((D_MODEL, t), BF16)],
        compiler_params=_cparams(("parallel",)),
        name="merge",
    )(attn2, ssm2, proj2, proj2, x2, wa, ws, wo, g_ffn)


def _extract_topk(s, k_top):
    r = s.shape[0]
    iota = lax.broadcasted_iota(jnp.int32, s.shape, 0).astype(F32)
    pos = jnp.full(s.shape, float(k_top), F32)
    vals = []
    for k in range(k_top):
        m = jnp.max(s, axis=0, keepdims=True)
        idx = jnp.min(jnp.where(s == m, iota, float(r)), axis=0, keepdims=True)
        hit = iota == idx
        pos = jnp.where(hit, float(k), pos)
        s = jnp.where(hit, -jnp.inf, s)
        vals.append(m)
    return vals, pos


def _extract_topk_distinct(s, k_top):
    s0 = s
    vals = []
    for k in range(k_top):
        m = jnp.max(s, axis=0, keepdims=True)
        s = jnp.where(s == m, -jnp.inf, s)
        vals.append(m)
    removed = jnp.sum(jnp.where(s == -jnp.inf, 1.0, 0.0), axis=0, keepdims=True)
    pos = jnp.zeros(s.shape, F32)
    for k in range(k_top):
        pos = jnp.where(vals[k] > s0, float(k + 1), pos)
    return vals, pos, removed


def _any_lane(flags):
    acc = flags[0]
    for f in flags[1:]:
        acc = acc | f
    return jnp.max(jnp.where(acc, 1.0, 0.0)) > 0.0


def _stack_rows(rows, n_rows):
    n = rows[0].shape[1]
    iota = lax.broadcasted_iota(jnp.int32, (n_rows, n), 0)
    out = jnp.zeros((n_rows, n), F32)
    for k, row in enumerate(rows):
        out = jnp.where(iota == k, row, out)
    return out


CAND_PAIRS = [(k1, k2) for k1 in range(PEER_TOPK) for k2 in range(PEER_TOPK) if (k1 + 1) * (k2 + 1) <= PEER_TOPK]
CAND_ROWS = -(-len(CAND_PAIRS) // 8) * 8
ROUTE_ILP = 4


def _route_kernel(hn_ref, wq_ref, sk_ref, cnt_ref, a_ref, pos_ref, b_ref, sc_scr, pos_scr, val_scr):
    k_top = PEER_TOPK
    tm = hn_ref.shape[0]
    nch = tm // LANES
    q = jnp.dot(hn_ref[...], wq_ref[...], preferred_element_type=F32).astype(BF16)
    for v in range(2 * PEER_HEADS):
        qv = q[:, v * HALF_KEY:(v + 1) * HALF_KEY]
        s = lax.dot_general(sk_ref[v], qv, (((1,), (1,)), ((), ())), preferred_element_type=F32)
        for ch in range(nch):
            sc_scr[v, ch] = s[:, ch * LANES:(ch + 1) * LANES]

    def stage1(i, carry):
        v, grp = i // (nch // ROUTE_ILP), i % (nch // ROUTE_ILP)
        chunks = tuple(ROUTE_ILP * grp + d for d in range(ROUTE_ILP))
        flags = []
        for ch in chunks:
            vals, pos, removed = _extract_topk_distinct(sc_scr[v, ch], k_top)
            pos_scr[v, ch] = pos
            val_scr[v, ch] = _stack_rows(vals, k_top)
            flags.append(removed != float(k_top))

        @pl.when(_any_lane(flags))
        def _():
            for ch in chunks:
                vals, pos = _extract_topk(sc_scr[v, ch], k_top)
                pos_scr[v, ch] = pos
                val_scr[v, ch] = _stack_rows(vals, k_top)

        return carry

    lax.fori_loop(0, 2 * PEER_HEADS * (nch // ROUTE_ILP), stage1, 0)

    def candidates(h, ch):
        v1 = val_scr[2 * h, ch]
        v2 = val_scr[2 * h + 1, ch]
        iota_c = lax.broadcasted_iota(jnp.int32, (CAND_ROWS, LANES), 0)
        cand = jnp.full((CAND_ROWS, LANES), -jnp.inf, F32)
        for r, (k1, k2) in enumerate(CAND_PAIRS):
            cand = jnp.where(iota_c == r, v1[k1:k1 + 1, :] + v2[k2:k2 + 1, :], cand)
        return cand

    def emit(h, ch, best, cpos):
        cs = slice(ch * LANES, (ch + 1) * LANES)
        sel = jnp.where(cpos < float(k_top), 1.0, 0.0)
        zsum = jnp.zeros((1, LANES), F32)
        for k in range(k_top):
            zsum = zsum + jnp.exp(best[k] - best[0])
        cnt_rows = [jnp.zeros((1, LANES), F32) for _ in range(k_top)]
        for r, (k1, k2) in enumerate(CAND_PAIRS):
            cnt_rows[k1] = cnt_rows[k1] + sel[r:r + 1, :]
        pos1 = pos_scr[2 * h, ch]
        cnt = jnp.zeros((N_KEYS, LANES), F32)
        for k in range(k_top):
            cnt = jnp.where(pos1 == float(k), cnt_rows[k], cnt)
        cnt_ref[h, :, cs] = cnt
        a_ref[h, :, cs] = jnp.exp(sc_scr[2 * h, ch] - val_scr[2 * h, ch, 0:1, :])
        pos_ref[h, :, cs] = pos_scr[2 * h + 1, ch]
        b_ref[h, :, cs] = jnp.exp(sc_scr[2 * h + 1, ch] - val_scr[2 * h + 1, ch, 0:1, :]) / zsum

    def stage2(h, carry):
        flags = []
        for ch in range(nch):
            best, cpos, removed = _extract_topk_distinct(candidates(h, ch), k_top)
            emit(h, ch, best, cpos)
            flags.append(removed != float(k_top + CAND_ROWS - len(CAND_PAIRS)))

        @pl.when(_any_lane(flags))
        def _():
            for ch in range(nch):
                best, cpos = _extract_topk(candidates(h, ch), k_top)
                emit(h, ch, best, cpos)

        return carry

    lax.fori_loop(0, PEER_HEADS, stage2, 0)


def _route(hn2, wq, sk, tm=512):
    t = hn2.shape[0]
    nch = tm // LANES
    nv = 2 * PEER_HEADS
    out = jax.ShapeDtypeStruct((PEER_HEADS, N_KEYS, t), F32)
    ospec = pl.BlockSpec((PEER_HEADS, N_KEYS, tm), lambda i: (0, 0, i))
    return pl.pallas_call(
        _route_kernel,
        grid=(t // tm,),
        in_specs=[
            pl.BlockSpec((tm, D_MODEL), lambda i: (i, 0)),
            _resident((D_MODEL, nv * HALF_KEY)),
            _resident((nv, N_KEYS, HALF_KEY)),
        ],
        out_specs=[ospec] * 4,
        out_shape=[out] * 4,
        scratch_shapes=[pltpu.VMEM((nv, nch, N_KEYS, LANES), F32), pltpu.VMEM((nv, nch, N_KEYS, LANES), F32),
                        pltpu.VMEM((nv, nch, PEER_TOPK, LANES), F32)],
        compiler_params=_cparams(("parallel",)),
        name="route",
    )(hn2, wq, sk)


SQRT_HALF = math.sqrt(0.5)


SUBLANES = 8
PEER_TE = SUBLANES * N_KEYS
PEER_SUB = 32
PEER_MM_ROWS = 512


def _peer_kernel(hnt_ref, u_ref, vt_ref, cnt_ref, a_ref, pos_ref, b_ref, x1_ref, gf_ref, y_ref,
                 acc_ref, ht_ref, p_ref):
    j = pl.program_id(1)
    te, tm = ht_ref.shape
    rows_per_tile = te // N_KEYS

    @pl.when(j == 0)
    def _():
        acc_ref[...] = jnp.zeros_like(acc_ref)

    for m0 in range(0, te, PEER_MM_ROWS):
        ms = slice(m0, m0 + PEER_MM_ROWS)
        ht_ref[ms, :] = jnp.dot(u_ref[ms, :], hnt_ref[...], preferred_element_type=F32)
    for c in range(tm // LANES):
        cs = slice(c * LANES, (c + 1) * LANES)
        for r in range(rows_per_tile):
            for s0 in range(0, N_KEYS, PEER_SUB):
                w = None
                for h in range(PEER_HEADS):
                    hk = slice(h * N_KEYS + s0, h * N_KEYS + s0 + PEER_SUB)
                    term = jnp.where(pos_ref[hk, cs] < cnt_ref[h, j, r:r + 1, cs],
                                     b_ref[hk, cs] * a_ref[h, j, r:r + 1, cs], 0.0)
                    w = term if w is None else w + term
                rs = slice(r * N_KEYS + s0, r * N_KEYS + s0 + PEER_SUB)
                hr = ht_ref[rs, cs]
                act = 0.5 * hr * (1.0 + lax.erf(hr * SQRT_HALF))
                p_ref[rs, cs] = (act * w).astype(BF16)
    for m0 in range(0, D_MODEL, PEER_MM_ROWS):
        ms = slice(m0, m0 + PEER_MM_ROWS)
        acc_ref[ms, :] += jnp.dot(vt_ref[ms, :], p_ref[...], preferred_element_type=F32)

    @pl.when(j == pl.num_programs(1) - 1)
    def _():
        y_ref[...] = _rms(x1_ref[...] + acc_ref[...].T, gf_ref[...])


def _peer(hnt, u_bf, vt_bf, route, x1, g_final, tm=512):
    t = hnt.shape[1]
    te = PEER_TE
    n_tiles = N_EXPERTS // te
    rows_per_tile = te // N_KEYS
    cnt, a, pos, b = route
    cnt = cnt.reshape(PEER_HEADS, n_tiles, rows_per_tile, t)
    a = a.reshape(PEER_HEADS, n_tiles, rows_per_tile, t)
    once = dict(pipeline_mode=pl.Buffered(1))
    pos = pos.reshape(PEER_HEADS * N_KEYS, t)
    b = b.reshape(PEER_HEADS * N_KEYS, t)
    rspec = pl.BlockSpec((PEER_HEADS * N_KEYS, tm), lambda i, j: (0, i), **once)
    tspec = pl.BlockSpec((PEER_HEADS, n_tiles, rows_per_tile, tm), lambda i, j: (0, 0, 0, i), **once)
    return pl.pallas_call(
        _peer_kernel,
        grid=(t // tm, n_tiles),
        in_specs=[
            pl.BlockSpec((D_MODEL, tm), lambda i, j: (0, i), **once),
            pl.BlockSpec((te, D_MODEL), lambda i, j: (j, 0)),
            pl.BlockSpec((None, D_MODEL, te), lambda i, j: (j, 0, 0)),
            tspec, tspec, rspec, rspec,
            pl.BlockSpec((tm, D_MODEL), lambda i, j: (i, 0), **once),
            pl.BlockSpec((1, D_MODEL), lambda i, j: (0, 0)),
        ],
        out_specs=pl.BlockSpec((tm, D_MODEL), lambda i, j: (i, 0)),
        out_shape=jax.ShapeDtypeStruct((t, D_MODEL), F32),
        scratch_shapes=[pltpu.VMEM((D_MODEL, tm), F32), pltpu.VMEM((te, tm), F32), pltpu.VMEM((te, tm), BF16)],
        compiler_params=_cparams(("parallel", "arbitrary")),
        name="peer",
    )(hnt, u_bf, vt_bf, cnt, a, pos, b, x1, g_final)


def _pad_lanes(v):
    return jnp.pad(v, (0, LANES - v.shape[0])).reshape(1, LANES)


def _prep(g_mix, w_in, attn_sink, conv_w, conv_b, a_log_f, a_log_b, dt_bias_f, dt_bias_b, d_skip, g_ssm_norm,
          w_attn_o, w_ssm_o, w_out, g_ffn, w_query, sub_keys, expert_u, expert_v, g_final):
    w = w_in[0].astype(BF16)
    w_packed = jnp.concatenate(
        [w[:, _V_END:_Z_END], w[:, _DT_END:_IN_W], w[:, _Z_END:_XBC_END], w[:, :_V_END], w[:, _XBC_END:_DT_END],
         jnp.zeros((D_MODEL, LANES - 2 * N_SSM_HEADS), w.dtype)], axis=1)
    slopes = jnp.exp2(-8.0 * jnp.arange(1, N_Q_HEADS + 1, dtype=F32) / N_Q_HEADS)
    return dict(
        g_mix=g_mix.reshape(1, D_MODEL),
        w_packed=w_packed,
        head_params=jnp.stack([slopes, attn_sink[0].astype(F32)]),
        conv_w=conv_w[0], conv_b=conv_b.reshape(1, XBC_W),
        alog_all=_pad_lanes(jnp.concatenate([a_log_f[0], a_log_b[0]])),
        dtb_all=_pad_lanes(jnp.concatenate([dt_bias_f[0], dt_bias_b[0]])),
        dskip=jnp.repeat(d_skip[0], SSM_HEAD_DIM).reshape(1, D_INNER),
        gnorm=g_ssm_norm.reshape(1, D_INNER),
        wa=w_attn_o[0].astype(BF16), ws=w_ssm_o[0].astype(BF16), wo=w_out[0].astype(BF16),
        g_ffn=g_ffn.reshape(1, D_MODEL),
        wq=w_query[0].astype(BF16),
        sk=sub_keys[0].reshape(2 * PEER_HEADS, N_KEYS, HALF_KEY).astype(BF16),
        u_bf=expert_u[0].astype(BF16),
        vt_bf=expert_v[0].astype(BF16).reshape(N_EXPERTS // PEER_TE, PEER_TE, D_MODEL).transpose(0, 2, 1),
        g_final=g_final.reshape(1, D_MODEL),
    )


def _trunk(x, p):
    b, s, d = x.shape
    t = b * s
    x2 = x.reshape(t, d)
    proj2 = _inproj(x2, p["g_mix"], p["w_packed"])
    proj3 = proj2.reshape(b, s, P_W)
    attn = _attention(proj3, p["head_params"])
    xbc_act = _conv_silu(proj3, p["conv_w"], p["conv_b"])
    y_f = _ssd(xbc_act, proj3, p["alog_all"], p["dtb_all"], reverse=False)
    ssm = _ssd(xbc_act, proj3, p["alog_all"], p["dtb_all"], reverse=True, extras=(y_f, p["dskip"], p["gnorm"]))
    x1, hn, hnt = _merge(attn.reshape(t, ATTN_W), ssm.reshape(t, D_INNER), proj2, x2,
                    p["wa"], p["ws"], p["wo"], p["g_ffn"])
    route = _route(hn, p["wq"], p["sk"])
    y = _peer(hnt, p["u_bf"], p["vt_bf"], route, x1, p["g_final"])
    return y.reshape(b, s, d)


def kernel(x_prompt, x_sample, g_mix, w_in, attn_sink, conv_w, conv_b, a_log_f, a_log_b, dt_bias_f, dt_bias_b,
           d_skip, g_ssm_norm, w_attn_o, w_ssm_o, w_out, g_ffn, w_query, sub_keys, expert_u, expert_v, g_final):
    p = _prep(g_mix, w_in, attn_sink, conv_w, conv_b, a_log_f, a_log_b, dt_bias_f, dt_bias_b, d_skip, g_ssm_norm,
              w_attn_o, w_ssm_o, w_out, g_ffn, w_query, sub_keys, expert_u, expert_v, g_final)
    return (_trunk(x_prompt, p), _trunk(x_sample, p))
```

```python
import functools
import math

import jax
import jax.numpy as jnp
from jax import lax
from jax.experimental import pallas as pl
from jax.experimental.pallas import tpu as pltpu

F32 = jnp.float32
BF16 = jnp.bfloat16

D_MODEL = 2048
N_Q_HEADS = 16
N_KV_HEADS = 4
HEAD_DIM = 64
Q_PER_KV = N_Q_HEADS // N_KV_HEADS
WINDOW = 128
BLOCK = 128
ATTN_W = N_Q_HEADS * HEAD_DIM
KV_W = N_KV_HEADS * HEAD_DIM
NEG_BIG = -1e30
D_INNER = D_MODEL
SSM_HEAD_DIM = 64
N_SSM_HEADS = D_INNER // SSM_HEAD_DIM
N_SSM_GROUPS = 4
D_STATE = 128
D_CONV = 5
CHUNK = 128
XBC_W = D_INNER + 2 * N_SSM_GROUPS * D_STATE
N_KEYS = 128
N_EXPERTS = N_KEYS * N_KEYS
PEER_HEADS = 8
PEER_TOPK = 16
HALF_KEY = 128
EPS = 1e-6

_Q_END = ATTN_W
_K_END = _Q_END + KV_W
_V_END = _K_END + KV_W
_Z_END = _V_END + D_INNER
_XBC_END = _Z_END + XBC_W
_DT_END = _XBC_END + 2 * N_SSM_HEADS
_IN_W = _DT_END + 2 * D_MODEL

LANES = 128
P_Z = 0
P_GA = P_Z + D_INNER
P_GS = P_GA + D_MODEL
P_XBC = P_GS + D_MODEL
P_Q = P_XBC + XBC_W
P_K = P_Q + ATTN_W
P_V = P_K + KV_W
P_DT = P_V + KV_W
P_W = P_DT + LANES

VMEM_LIMIT = 56 * 1024 * 1024


def _cparams(sem, flags=None):
    return pltpu.CompilerParams(dimension_semantics=sem, vmem_limit_bytes=VMEM_LIMIT, flags=flags)


def _sigmoid(x):
    return 1.0 / (1.0 + jnp.exp(-x))


def _rms(x, g):
    ms = jnp.mean(x * x, axis=-1, keepdims=True)
    return (x * lax.rsqrt(ms + EPS)) * g


def _inproj_kernel(x_ref, g_ref, w_ref, o_ref, h_scr):
    @pl.when(pl.program_id(1) == 0)
    def _():
        h_scr[...] = _rms(x_ref[...], g_ref[...]).astype(BF16)

    o_ref[...] = jnp.dot(h_scr[...], w_ref[...], preferred_element_type=F32)


def _inproj(x2, g_mix, w_packed, tm=512, tn=2176):
    t = x2.shape[0]
    return pl.pallas_call(
        _inproj_kernel,
        grid=(t // tm, P_W // tn),
        in_specs=[
            pl.BlockSpec((tm, D_MODEL), lambda i, j: (i, 0)),
            pl.BlockSpec((1, D_MODEL), lambda i, j: (0, 0)),
            pl.BlockSpec((D_MODEL, tn), lambda i, j: (0, j)),
        ],
        out_specs=pl.BlockSpec((tm, tn), lambda i, j: (i, j)),
        out_shape=jax.ShapeDtypeStruct((t, P_W), F32),
        scratch_shapes=[pltpu.VMEM((tm, D_MODEL), BF16)],
        compiler_params=_cparams(("parallel", "arbitrary")),
        name="inproj",
    )(x2, g_mix, w_packed)


ATTN_ROWS = 32


def _attn_kernel(hp_ref, q_ref, kp_ref, kc_ref, kn_ref, vp_ref, vc_ref, vn_ref, o_ref, s_scr, p_scr, *, seq):
    j = pl.program_id(1)
    q = (q_ref[...] * (HEAD_DIM ** -0.5)).astype(BF16)
    kw = jnp.concatenate([kp_ref[...], kc_ref[...], kn_ref[...]], axis=0).astype(BF16)
    vw = jnp.concatenate([vp_ref[...], vc_ref[...], vn_ref[...]], axis=0).astype(BF16)
    qi = lax.broadcasted_iota(jnp.int32, (BLOCK, 3 * BLOCK), 0)
    km = lax.broadcasted_iota(jnp.int32, (BLOCK, 3 * BLOCK), 1)
    rel = qi - km + BLOCK
    key_pos = (j - 1) * BLOCK + km
    valid = (jnp.abs(rel) <= WINDOW) & (key_pos >= 0) & (key_pos < seq)
    dmask = jnp.where(valid, jnp.abs(rel).astype(F32), jnp.inf)

    for kv in range(N_KV_HEADS):
        qg = jnp.concatenate([q[:, (kv * Q_PER_KV + g) * HEAD_DIM:(kv * Q_PER_KV + g + 1) * HEAD_DIM]
                              for g in range(Q_PER_KV)], axis=0)
        kh = kw[:, kv * HEAD_DIM:(kv + 1) * HEAD_DIM]
        s_scr[kv] = lax.dot_general(qg, kh, (((1,), (1,)), ((), ())), preferred_element_type=F32)

    for h in range(N_Q_HEADS):
        kv, g = divmod(h, Q_PER_KV)
        sink = hp_ref[1, h]
        bias = hp_ref[0, h] * dmask
        for r0 in range(0, BLOCK, ATTN_ROWS):
            rows = slice(g * BLOCK + r0, g * BLOCK + r0 + ATTN_ROWS)
            s = s_scr[kv, rows, :] - bias[r0:r0 + ATTN_ROWS, :]
            mx = jnp.maximum(jnp.max(s, axis=-1, keepdims=True), sink)
            p = jnp.exp(s - mx)
            denom = jnp.sum(p, axis=-1, keepdims=True) + jnp.exp(sink - mx)
            p_scr[kv, rows, :] = (p / denom).astype(BF16)

    outs = []
    for kv in range(N_KV_HEADS):
        o = jnp.dot(p_scr[kv], vw[:, kv * HEAD_DIM:(kv + 1) * HEAD_DIM], preferred_element_type=F32)
        outs += [o[g * BLOCK:(g + 1) * BLOCK, :] for g in range(Q_PER_KV)]
    o_ref[...] = jnp.concatenate(outs, axis=-1).astype(o_ref.dtype)


def _attention(proj3, head_params):
    b, s, _ = proj3.shape
    nb = s // BLOCK
    kcol, vcol = P_K // KV_W, P_V // KV_W
    prev = lambda bb, j: jnp.maximum(j - 1, 0)
    nxt = lambda bb, j: jnp.minimum(j + 1, nb - 1)
    kv_spec = lambda col, f: pl.BlockSpec((None, BLOCK, KV_W), lambda bb, j: (bb, f(bb, j), col))
    cur = lambda bb, j: j
    return pl.pallas_call(
        functools.partial(_attn_kernel, seq=s),
        grid=(b, nb),
        in_specs=[
            pl.BlockSpec(memory_space=pltpu.SMEM),
            pl.BlockSpec((None, BLOCK, ATTN_W), lambda bb, j: (bb, j, P_Q // ATTN_W)),
            kv_spec(kcol, prev), kv_spec(kcol, cur), kv_spec(kcol, nxt),
            kv_spec(vcol, prev), kv_spec(vcol, cur), kv_spec(vcol, nxt),
        ],
        out_specs=pl.BlockSpec((None, BLOCK, ATTN_W), lambda bb, j: (bb, j, 0)),
        out_shape=jax.ShapeDtypeStruct((b, s, ATTN_W), BF16),
        scratch_shapes=[pltpu.VMEM((N_KV_HEADS, Q_PER_KV * BLOCK, 3 * BLOCK), F32),
                        pltpu.VMEM((N_KV_HEADS, Q_PER_KV * BLOCK, 3 * BLOCK), BF16)],
        compiler_params=_cparams(("parallel", "arbitrary")),
        name="attn",
    )(head_params, proj3, proj3, proj3, proj3, proj3, proj3, proj3)


HALO = 8
CONV_TB = 512
CONV_CB = 512


def _conv_kernel(cur_ref, prev_ref, next_ref, w_ref, b_ref, o_ref, ext):
    i = pl.program_id(1)
    last = pl.num_programs(1) - 1
    tb = cur_ref.shape[0]
    ext[0:HALO, :] = jnp.where(i == 0, 0.0, prev_ref[...])
    ext[HALO:HALO + tb, :] = cur_ref[...]
    ext[HALO + tb:, :] = jnp.where(i == last, 0.0, next_ref[...])
    half = D_CONV // 2
    acc = b_ref[...] + w_ref[0:1, :] * ext[HALO - half:HALO - half + tb, :]
    for k in range(1, D_CONV):
        acc = acc + w_ref[k:k + 1, :] * ext[HALO - half + k:HALO - half + k + tb, :]
    o_ref[...] = acc * _sigmoid(acc)


def _conv_silu(proj3, conv_w, conv_b):
    b, s, _ = proj3.shape
    tb, cb = CONV_TB, CONV_CB
    c0 = P_XBC // cb
    nrow8 = s // HALO
    return pl.pallas_call(
        _conv_kernel,
        grid=(b, s // tb, XBC_W // cb),
        in_specs=[
            pl.BlockSpec((None, tb, cb), lambda bb, i, c: (bb, i, c0 + c)),
            pl.BlockSpec((None, HALO, cb), lambda bb, i, c: (bb, jnp.maximum(i * (tb // HALO) - 1, 0), c0 + c)),
            pl.BlockSpec((None, HALO, cb),
                         lambda bb, i, c: (bb, jnp.minimum((i + 1) * (tb // HALO), nrow8 - 1), c0 + c)),
            pl.BlockSpec((D_CONV, cb), lambda bb, i, c: (0, c)),
            pl.BlockSpec((1, cb), lambda bb, i, c: (0, c)),
        ],
        out_specs=pl.BlockSpec((None, tb, cb), lambda bb, i, c: (bb, i, c)),
        out_shape=jax.ShapeDtypeStruct((b, s, XBC_W), F32),
        scratch_shapes=[pltpu.VMEM((tb + 2 * HALO, cb), F32)],
        compiler_params=_cparams(("parallel", "arbitrary", "arbitrary")),
        name="conv",
    )(proj3, proj3, proj3, conv_w, conv_b)


N_PAIRS = N_SSM_HEADS // 2
PAIRS_PER_GROUP = N_PAIRS // N_SSM_GROUPS
GROUP_W = D_INNER // N_SSM_GROUPS


def _softplus(x):
    return jnp.maximum(x, 0.0) + jnp.log1p(jnp.exp(-jnp.abs(x)))


def _ssd_kernel(*refs, reverse):
    if reverse:
        xbc_ref, dt_ref, alog_ref, dtb_ref, yf_ref, z_ref, dskip_ref, gn_ref, o_ref, state, ybuf = refs
    else:
        xbc_ref, dt_ref, alog_ref, dtb_ref, o_ref, state = refs
        ybuf = o_ref
    L = CHUNK

    @pl.when(pl.program_id(1) == 0)
    def _():
        state[...] = jnp.zeros_like(state)

    dt = _softplus(dt_ref[...] + dtb_ref[...])
    dta = dt * (-jnp.exp(alog_ref[...]))
    row = lax.broadcasted_iota(jnp.int32, (L, L), 0)
    col = lax.broadcasted_iota(jnp.int32, (L, L), 1)
    tri = (col >= row) if reverse else (col <= row)
    cum = jnp.dot(tri.astype(F32), dta, precision=lax.Precision.HIGHEST, preferred_element_type=F32)
    cum_t = cum.T
    dt_t = dt.T
    end = 0 if reverse else L - 1
    ecum = jnp.exp(cum)
    left = lax.broadcasted_iota(jnp.int32, (L, LANES), 1) < SSM_HEAD_DIM
    off = N_SSM_HEADS if reverse else 0

    for g in range(N_SSM_GROUPS):
        bg = xbc_ref[:, D_INNER + g * D_STATE:D_INNER + (g + 1) * D_STATE]
        cg = xbc_ref[:, D_INNER + (N_SSM_GROUPS + g) * D_STATE:D_INNER + (N_SSM_GROUPS + g + 1) * D_STATE]
        cgb = cg.astype(BF16)
        cb = lax.dot_general(cgb, bg.astype(BF16), (((1,), (1,)), ((), ())),
                             preferred_element_type=F32)
        bt = bg.T
        for pp in range(PAIRS_PER_GROUP):
            hp = g * PAIRS_PER_GROUP + pp
            h1, h2 = 2 * hp + off, 2 * hp + 1 + off
            xb = xbc_ref[:, hp * LANES:(hp + 1) * LANES].astype(BF16)
            zero = jnp.zeros_like(xb)
            xbd = jnp.concatenate([jnp.where(left, xb, zero), jnp.where(left, zero, xb)], axis=0)
            ms, bds, decs = [], [], []
            for h in (h1, h2):
                seg = cum[:, h:h + 1] - cum_t[h:h + 1, :]
                lmat = jnp.exp(jnp.where(tri, seg, NEG_BIG))
                ms.append((cb * lmat * dt_t[h:h + 1, :]).astype(BF16))
                tot = cum_t[h:h + 1, end:end + 1]
                bds.append((bt * (jnp.exp(tot - cum_t[h:h + 1, :]) * dt_t[h:h + 1, :])).astype(BF16))
                decs.append(jnp.exp(tot))
            sp = state[hp]
            y = jnp.dot(jnp.concatenate(ms, axis=1), xbd, preferred_element_type=F32)
            y_off = jnp.dot(cgb, sp.astype(BF16), preferred_element_type=F32)
            y = y + y_off * jnp.where(left, ecum[:, h1:h1 + 1], ecum[:, h2:h2 + 1])
            ybuf[:, hp * LANES:(hp + 1) * LANES] = y
            snew = jnp.dot(jnp.concatenate(bds, axis=1), xbd, preferred_element_type=F32)
            state[hp] = sp * jnp.where(left, decs[0], decs[1]) + snew

    if reverse:
        y = ybuf[...] + yf_ref[...] + dskip_ref[...] * xbc_ref[:, 0:D_INNER]
        z = z_ref[...]
        gated = y * (z * _sigmoid(z))
        for g in range(N_SSM_GROUPS):
            sl = slice(g * GROUP_W, (g + 1) * GROUP_W)
            o_ref[:, sl] = _rms(gated[:, sl], gn_ref[:, sl]).astype(o_ref.dtype)


def _ssd(xbc_act, proj3, alog_all, dtb_all, reverse, extras=()):
    b, s, _ = xbc_act.shape
    nc = s // CHUNK
    cidx = (lambda c: nc - 1 - c) if reverse else (lambda c: c)
    row_spec = lambda w, colblk: pl.BlockSpec((None, CHUNK, w), lambda bb, c: (bb, cidx(c), colblk))
    par_spec = lambda w: pl.BlockSpec((1, w), lambda bb, c: (0, 0))
    in_specs = [row_spec(XBC_W, 0), row_spec(LANES, P_DT // LANES), par_spec(LANES), par_spec(LANES)]
    args = [xbc_act, proj3, alog_all, dtb_all]
    scratch = [pltpu.VMEM((N_PAIRS, D_STATE, LANES), F32)]
    if reverse:
        yf, dskip, gnorm = extras
        in_specs += [row_spec(D_INNER, 0), row_spec(D_INNER, P_Z // D_INNER), par_spec(D_INNER), par_spec(D_INNER)]
        args += [yf, proj3, dskip, gnorm]
        scratch.append(pltpu.VMEM((CHUNK, D_INNER), F32))
        out_dtype = BF16
    else:
        out_dtype = F32
    return pl.pallas_call(
        functools.partial(_ssd_kernel, reverse=reverse),
        grid=(b, nc),
        in_specs=in_specs,
        out_specs=row_spec(D_INNER, 0),
        out_shape=jax.ShapeDtypeStruct((b, s, D_INNER), out_dtype),
        scratch_shapes=scratch,
        compiler_params=_cparams(("parallel", "arbitrary")),
        name="ssd_bwd" if reverse else "ssd_fwd",
    )(*args)


def _merge_kernel(attn_ref, ssm_ref, ga_ref, gs_ref, x_ref, wa_ref, ws_ref, wo_ref, gf_ref, x1_ref, hn_ref):
    a = jnp.dot(attn_ref[...], wa_ref[...], preferred_element_type=F32)
    s = jnp.dot(ssm_ref[...], ws_ref[...], preferred_element_type=F32)
    merged = _sigmoid(ga_ref[...]) * a + _sigmoid(gs_ref[...]) * s
    x1 = x_ref[...] + jnp.dot(merged.astype(BF16), wo_ref[...], preferred_element_type=F32)
    x1_ref[...] = x1
    hn_ref[...] = _rms(x1, gf_ref[...]).astype(BF16)


def _resident(shape):
    return pl.BlockSpec(shape, lambda *_: (0,) * len(shape), pipeline_mode=pl.Buffered(1))


def _merge(attn2, ssm2, proj2, x2, wa, ws, wo, g_ffn, tm=256):
    t = x2.shape[0]
    rows = lambda w, colblk=0: pl.BlockSpec((tm, w), lambda i: (i, colblk))
    return pl.pallas_call(
        _merge_kernel,
        grid=(t // tm,),
        in_specs=[
            rows(ATTN_W), rows(D_INNER), rows(D_MODEL, P_GA // D_MODEL), rows(D_MODEL, P_GS // D_MODEL),
            rows(D_MODEL), _resident((ATTN_W, D_MODEL)), _resident((D_INNER, D_MODEL)),
            _resident((D_MODEL, D_MODEL)), _resident((1, D_MODEL)),
        ],
        out_specs=[rows(D_MODEL), rows(D_MODEL)],
        out_shape=[jax.ShapeDtypeStruct((t, D_MODEL), F32), jax.ShapeDtypeStruct((t, D_MODEL), BF16)],
        compiler_params=_cparams(("parallel",)),
        name="merge",
    )(attn2, ssm2, proj2, proj2, x2, wa, ws, wo, g_ffn)


def _extract_topk(s, k_top):
    r = s.shape[0]
    iota = lax.broadcasted_iota(jnp.int32, s.shape, 0).astype(F32)
    pos = jnp.full(s.shape, float(k_top), F32)
    vals = []
    for k in range(k_top):
        m = jnp.max(s, axis=0, keepdims=True)
        idx = jnp.min(jnp.where(s == m, iota, float(r)), axis=0, keepdims=True)
        hit = iota == idx
        pos = jnp.where(hit, float(k), pos)
        s = jnp.where(hit, -jnp.inf, s)
        vals.append(m)
    return vals, pos


def _extract_topk_distinct(s, k_top):
    s0 = s
    vals = []
    for k in range(k_top):
        m = jnp.max(s, axis=0, keepdims=True)
        s = jnp.where(s == m, -jnp.inf, s)
        vals.append(m)
    removed = jnp.sum(jnp.where(s == -jnp.inf, 1.0, 0.0), axis=0, keepdims=True)
    pos = jnp.zeros(s.shape, F32)
    for k in range(k_top):
        pos = jnp.where(vals[k] > s0, float(k + 1), pos)
    return vals, pos, removed


def _any_lane(flags):
    acc = flags[0]
    for f in flags[1:]:
        acc = acc | f
    return jnp.max(jnp.where(acc, 1.0, 0.0)) > 0.0


def _stack_rows(rows, n_rows):
    n = rows[0].shape[1]
    iota = lax.broadcasted_iota(jnp.int32, (n_rows, n), 0)
    out = jnp.zeros((n_rows, n), F32)
    for k, row in enumerate(rows):
        out = jnp.where(iota == k, row, out)
    return out


CAND_PAIRS = [(k1, k2) for k1 in range(PEER_TOPK) for k2 in range(PEER_TOPK) if (k1 + 1) * (k2 + 1) <= PEER_TOPK]
CAND_ROWS = -(-len(CAND_PAIRS) // 8) * 8
ROUTE_ILP = 4


def _route_kernel(hn_ref, wq_ref, sk_ref, cnt_ref, a_ref, pos_ref, b_ref, sc_scr, pos_scr, val_scr):
    k_top = PEER_TOPK
    tm = hn_ref.shape[0]
    nch = tm // LANES
    q = jnp.dot(hn_ref[...], wq_ref[...], preferred_element_type=F32).astype(BF16)
    for v in range(2 * PEER_HEADS):
        qv = q[:, v * HALF_KEY:(v + 1) * HALF_KEY]
        s = lax.dot_general(sk_ref[v], qv, (((1,), (1,)), ((), ())), preferred_element_type=F32)
        for ch in range(nch):
            sc_scr[v, ch] = s[:, ch * LANES:(ch + 1) * LANES]

    def stage1(i, carry):
        v, grp = i // (nch // ROUTE_ILP), i % (nch // ROUTE_ILP)
        chunks = tuple(ROUTE_ILP * grp + d for d in range(ROUTE_ILP))
        flags = []
        for ch in chunks:
            vals, pos, removed = _extract_topk_distinct(sc_scr[v, ch], k_top)
            pos_scr[v, ch] = pos
            val_scr[v, ch] = _stack_rows(vals, k_top)
            flags.append(removed != float(k_top))

        @pl.when(_any_lane(flags))
        def _():
            for ch in chunks:
                vals, pos = _extract_topk(sc_scr[v, ch], k_top)
                pos_scr[v, ch] = pos
                val_scr[v, ch] = _stack_rows(vals, k_top)

        return carry

    lax.fori_loop(0, 2 * PEER_HEADS * (nch // ROUTE_ILP), stage1, 0)

    def candidates(h, ch):
        v1 = val_scr[2 * h, ch]
        v2 = val_scr[2 * h + 1, ch]
        iota_c = lax.broadcasted_iota(jnp.int32, (CAND_ROWS, LANES), 0)
        cand = jnp.full((CAND_ROWS, LANES), -jnp.inf, F32)
        for r, (k1, k2) in enumerate(CAND_PAIRS):
            cand = jnp.where(iota_c == r, v1[k1:k1 + 1, :] + v2[k2:k2 + 1, :], cand)
        return cand

    def emit(h, ch, best, cpos):
        cs = slice(ch * LANES, (ch + 1) * LANES)
        sel = jnp.where(cpos < float(k_top), 1.0, 0.0)
        zsum = jnp.zeros((1, LANES), F32)
        for k in range(k_top):
            zsum = zsum + jnp.exp(best[k] - best[0])
        cnt_rows = [jnp.zeros((1, LANES), F32) for _ in range(k_top)]
        for r, (k1, k2) in enumerate(CAND_PAIRS):
            cnt_rows[k1] = cnt_rows[k1] + sel[r:r + 1, :]
        pos1 = pos_scr[2 * h, ch]
        cnt = jnp.zeros((N_KEYS, LANES), F32)
        for k in range(k_top):
            cnt = jnp.where(pos1 == float(k), cnt_rows[k], cnt)
        cnt_ref[h, :, cs] = cnt
        a_ref[h, :, cs] = jnp.exp(sc_scr[2 * h, ch] - val_scr[2 * h, ch, 0:1, :])
        pos_ref[h, :, cs] = pos_scr[2 * h + 1, ch]
        b_ref[h, :, cs] = jnp.exp(sc_scr[2 * h + 1, ch] - val_scr[2 * h + 1, ch, 0:1, :]) / zsum

    def stage2(h, carry):
        flags = []
        for ch in range(nch):
            best, cpos, removed = _extract_topk_distinct(candidates(h, ch), k_top)
            emit(h, ch, best, cpos)
            flags.append(removed != float(k_top + CAND_ROWS - len(CAND_PAIRS)))

        @pl.when(_any_lane(flags))
        def _():
            for ch in range(nch):
                best, cpos = _extract_topk(candidates(h, ch), k_top)
                emit(h, ch, best, cpos)

        return carry

    lax.fori_loop(0, PEER_HEADS, stage2, 0)


def _route(hn2, wq, sk, tm=512):
    t = hn2.shape[0]
    nch = tm // LANES
    nv = 2 * PEER_HEADS
    out = jax.ShapeDtypeStruct((PEER_HEADS, N_KEYS, t), F32)
    ospec = pl.BlockSpec((PEER_HEADS, N_KEYS, tm), lambda i: (0, 0, i))
    return pl.pallas_call(
        _route_kernel,
        grid=(t // tm,),
        in_specs=[
            pl.BlockSpec((tm, D_MODEL), lambda i: (i, 0)),
            _resident((D_MODEL, nv * HALF_KEY)),
            _resident((nv, N_KEYS, HALF_KEY)),
        ],
        out_specs=[ospec] * 4,
        out_shape=[out] * 4,
        scratch_shapes=[pltpu.VMEM((nv, nch, N_KEYS, LANES), F32), pltpu.VMEM((nv, nch, N_KEYS, LANES), F32),
                        pltpu.VMEM((nv, nch, PEER_TOPK, LANES), F32)],
        compiler_params=_cparams(("parallel",)),
        name="route",
    )(hn2, wq, sk)


SQRT_HALF = math.sqrt(0.5)
SUBLANES = 8
PEER_TE = SUBLANES * N_KEYS
PEER_SUB = 64
PEER_MM_ROWS = 512
BF16_ROWS = 16


def _packed_rows(row):
    one = jnp.broadcast_to(row, (BF16_ROWS, LANES)).astype(BF16)
    return jnp.concatenate([one] * (PEER_SUB // BF16_ROWS), axis=0)


def _peer_kernel(hn_ref, u_ref, vt_ref, cnt_ref, a_ref, pos_ref, b_ref, x1_ref, gf_ref, y_ref,
                 acc_ref, ht_ref, p_ref, pos_bf, b_bf):
    j = pl.program_id(1)
    te, tm = ht_ref.shape
    rows_per_tile = te // N_KEYS

    @pl.when(j == 0)
    def _():
        acc_ref[...] = jnp.zeros_like(acc_ref)
        pos_bf[...] = pos_ref[...].astype(BF16)
        b_bf[...] = b_ref[...].astype(BF16)

    for m0 in range(0, te, PEER_MM_ROWS):
        ms = slice(m0, m0 + PEER_MM_ROWS)
        ht_ref[ms, :] = lax.dot_general(u_ref[ms, :], hn_ref[...], (((1,), (1,)), ((), ())),
                                        preferred_element_type=F32)
    zero = jnp.zeros((PEER_SUB, LANES), BF16)
    for c in range(tm // LANES):
        cs = slice(c * LANES, (c + 1) * LANES)
        for r in range(rows_per_tile):
            cnt_rows = [_packed_rows(cnt_ref[h, j, r:r + 1, cs]) for h in range(PEER_HEADS)]
            a_rows = [_packed_rows(a_ref[h, j, r:r + 1, cs]) for h in range(PEER_HEADS)]
            for s0 in range(0, N_KEYS, PEER_SUB):
                w = None
                for h in range(PEER_HEADS):
                    hk = slice(h * N_KEYS + s0, h * N_KEYS + s0 + PEER_SUB)
                    term = jnp.where(pos_bf[hk, cs] < cnt_rows[h], b_bf[hk, cs] * a_rows[h], zero)
                    w = term if w is None else w + term
                rs = slice(r * N_KEYS + s0, r * N_KEYS + s0 + PEER_SUB)
                hr = ht_ref[rs, cs]
                act = 0.5 * hr * (1.0 + lax.erf(hr * SQRT_HALF))
                p_ref[rs, cs] = act.astype(BF16) * w
    for m0 in range(0, D_MODEL, PEER_MM_ROWS):
        ms = slice(m0, m0 + PEER_MM_ROWS)
        acc_ref[ms, :] += jnp.dot(vt_ref[ms, :], p_ref[...], preferred_element_type=F32)

    @pl.when(j == pl.num_programs(1) - 1)
    def _():
        y_ref[...] = _rms(x1_ref[...] + acc_ref[...].T, gf_ref[...])


def _peer(hn2, u_bf, vt_bf, route, x1, g_final, tm=512):
    t = hn2.shape[0]
    te = PEER_TE
    n_tiles = N_EXPERTS // te
    rows_per_tile = te // N_KEYS
    cnt, a, pos, b = route
    cnt = cnt.reshape(PEER_HEADS, n_tiles, rows_per_tile, t)
    a = a.reshape(PEER_HEADS, n_tiles, rows_per_tile, t)
    once = dict(pipeline_mode=pl.Buffered(1))
    pos = pos.reshape(PEER_HEADS * N_KEYS, t)
    b = b.reshape(PEER_HEADS * N_KEYS, t)
    rspec = pl.BlockSpec((PEER_HEADS * N_KEYS, tm), lambda i, j: (0, i), **once)
    tspec = pl.BlockSpec((PEER_HEADS, n_tiles, rows_per_tile, tm), lambda i, j: (0, 0, 0, i), **once)
    return pl.pallas_call(
        _peer_kernel,
        grid=(t // tm, n_tiles),
        in_specs=[
            pl.BlockSpec((tm, D_MODEL), lambda i, j: (i, 0), **once),
            pl.BlockSpec((te, D_MODEL), lambda i, j: (j, 0)),
            pl.BlockSpec((None, D_MODEL, te), lambda i, j: (j, 0, 0)),
            tspec, tspec, rspec, rspec,
            pl.BlockSpec((tm, D_MODEL), lambda i, j: (i, 0), **once),
            pl.BlockSpec((1, D_MODEL), lambda i, j: (0, 0)),
        ],
        out_specs=pl.BlockSpec((tm, D_MODEL), lambda i, j: (i, 0)),
        out_shape=jax.ShapeDtypeStruct((t, D_MODEL), F32),
        scratch_shapes=[pltpu.VMEM((D_MODEL, tm), F32), pltpu.VMEM((te, tm), F32), pltpu.VMEM((te, tm), BF16),
                        pltpu.VMEM((PEER_HEADS * N_KEYS, tm), BF16), pltpu.VMEM((PEER_HEADS * N_KEYS, tm), BF16)],
        compiler_params=_cparams(("parallel", "arbitrary")),
        name="peer",
    )(hn2, u_bf, vt_bf, cnt, a, pos, b, x1, g_final)


def _pad_lanes(v):
    return jnp.pad(v, (0, LANES - v.shape[0])).reshape(1, LANES)


def _prep(g_mix, w_in, attn_sink, conv_w, conv_b, a_log_f, a_log_b, dt_bias_f, dt_bias_b, d_skip, g_ssm_norm,
          w_attn_o, w_ssm_o, w_out, g_ffn, w_query, sub_keys, expert_u, expert_v, g_final):
    w = w_in[0].astype(BF16)
    w_packed = jnp.concatenate(
        [w[:, _V_END:_Z_END], w[:, _DT_END:_IN_W], w[:, _Z_END:_XBC_END], w[:, :_V_END], w[:, _XBC_END:_DT_END],
         jnp.zeros((D_MODEL, LANES - 2 * N_SSM_HEADS), w.dtype)], axis=1)
    slopes = jnp.exp2(-8.0 * jnp.arange(1, N_Q_HEADS + 1, dtype=F32) / N_Q_HEADS)
    return dict(
        g_mix=g_mix.reshape(1, D_MODEL),
        w_packed=w_packed,
        head_params=jnp.stack([slopes, attn_sink[0].astype(F32)]),
        conv_w=conv_w[0], conv_b=conv_b.reshape(1, XBC_W),
        alog_all=_pad_lanes(jnp.concatenate([a_log_f[0], a_log_b[0]])),
        dtb_all=_pad_lanes(jnp.concatenate([dt_bias_f[0], dt_bias_b[0]])),
        dskip=jnp.repeat(d_skip[0], SSM_HEAD_DIM).reshape(1, D_INNER),
        gnorm=g_ssm_norm.reshape(1, D_INNER),
        wa=w_attn_o[0].astype(BF16), ws=w_ssm_o[0].astype(BF16), wo=w_out[0].astype(BF16),
        g_ffn=g_ffn.reshape(1, D_MODEL),
        wq=w_query[0].astype(BF16),
        sk=sub_keys[0].reshape(2 * PEER_HEADS, N_KEYS, HALF_KEY).astype(BF16),
        u_bf=expert_u[0].astype(BF16),
        vt_bf=expert_v[0].astype(BF16).reshape(N_EXPERTS // PEER_TE, PEER_TE, D_MODEL).transpose(0, 2, 1),
        g_final=g_final.reshape(1, D_MODEL),
    )


def _trunk(x, p):
    b, s, d = x.shape
    t = b * s
    x2 = x.reshape(t, d)
    proj2 = _inproj(x2, p["g_mix"], p["w_packed"])
    proj3 = proj2.reshape(b, s, P_W)
    attn = _attention(proj3, p["head_params"])
    xbc_act = _conv_silu(proj3, p["conv_w"], p["conv_b"])
    y_f = _ssd(xbc_act, proj3, p["alog_all"], p["dtb_all"], reverse=False)
    ssm = _ssd(xbc_act, proj3, p["alog_all"], p["dtb_all"], reverse=True, extras=(y_f, p["dskip"], p["gnorm"]))
    x1, hn = _merge(attn.reshape(t, ATTN_W), ssm.reshape(t, D_INNER), proj2, x2,
                    p["wa"], p["ws"], p["wo"], p["g_ffn"])
    route = _route(hn, p["wq"], p["sk"])
    y = _peer(hn, p["u_bf"], p["vt_bf"], route, x1, p["g_final"])
    return y.reshape(b, s, d)


def kernel(x_prompt, x_sample, g_mix, w_in, attn_sink, conv_w, conv_b, a_log_f, a_log_b, dt_bias_f, dt_bias_b,
           d_skip, g_ssm_norm, w_attn_o, w_ssm_o, w_out, g_ffn, w_query, sub_keys, expert_u, expert_v, g_final):
    p = _prep(g_mix, w_in, attn_sink, conv_w, conv_b, a_log_f, a_log_b, dt_bias_f, dt_bias_b, d_skip, g_ssm_norm,
              w_attn_o, w_ssm_o, w_out, g_ffn, w_query, sub_keys, expert_u, expert_v, g_final)
    return (_trunk(x_prompt, p), _trunk(x_sample, p))
```

```python
import functools
import math

import jax
import jax.numpy as jnp
from jax import lax
from jax.experimental import pallas as pl
from jax.experimental.pallas import tpu as pltpu

F32 = jnp.float32
BF16 = jnp.bfloat16

D_MODEL = 2048
N_Q_HEADS = 16
N_KV_HEADS = 4
HEAD_DIM = 64
Q_PER_KV = N_Q_HEADS // N_KV_HEADS
WINDOW = 128
BLOCK = 128
ATTN_W = N_Q_HEADS * HEAD_DIM
KV_W = N_KV_HEADS * HEAD_DIM
NEG_BIG = -1e30
D_INNER = D_MODEL
SSM_HEAD_DIM = 64
N_SSM_HEADS = D_INNER // SSM_HEAD_DIM
N_SSM_GROUPS = 4
D_STATE = 128
D_CONV = 5
CHUNK = 128
XBC_W = D_INNER + 2 * N_SSM_GROUPS * D_STATE
N_KEYS = 128
N_EXPERTS = N_KEYS * N_KEYS
PEER_HEADS = 8
PEER_TOPK = 16
HALF_KEY = 128
EPS = 1e-6

_Q_END = ATTN_W
_K_END = _Q_END + KV_W
_V_END = _K_END + KV_W
_Z_END = _V_END + D_INNER
_XBC_END = _Z_END + XBC_W
_DT_END = _XBC_END + 2 * N_SSM_HEADS
_IN_W = _DT_END + 2 * D_MODEL

LANES = 128
P_Z = 0
P_GA = P_Z + D_INNER
P_GS = P_GA + D_MODEL
P_XBC = P_GS + D_MODEL
P_Q = P_XBC + XBC_W
P_K = P_Q + ATTN_W
P_V = P_K + KV_W
P_DT = P_V + KV_W
P_W = P_DT + LANES

VMEM_LIMIT = 56 * 1024 * 1024


def _cparams(sem, flags=None):
    return pltpu.CompilerParams(dimension_semantics=sem, vmem_limit_bytes=VMEM_LIMIT, flags=flags)


def _sigmoid(x):
    return 1.0 / (1.0 + jnp.exp(-x))


def _rms(x, g):
    ms = jnp.mean(x * x, axis=-1, keepdims=True)
    return (x * lax.rsqrt(ms + EPS)) * g


def _inproj_kernel(x_ref, g_ref, w_ref, o_ref, h_scr):
    @pl.when(pl.program_id(1) == 0)
    def _():
        h_scr[...] = _rms(x_ref[...], g_ref[...]).astype(BF16)

    o_ref[...] = jnp.dot(h_scr[...], w_ref[...], preferred_element_type=F32)


def _inproj(x2, g_mix, w_packed, tm=512, tn=2176):
    t = x2.shape[0]
    return pl.pallas_call(
        _inproj_kernel,
        grid=(t // tm, P_W // tn),
        in_specs=[
            pl.BlockSpec((tm, D_MODEL), lambda i, j: (i, 0)),
            pl.BlockSpec((1, D_MODEL), lambda i, j: (0, 0)),
            pl.BlockSpec((D_MODEL, tn), lambda i, j: (0, j)),
        ],
        out_specs=pl.BlockSpec((tm, tn), lambda i, j: (i, j)),
        out_shape=jax.ShapeDtypeStruct((t, P_W), F32),
        scratch_shapes=[pltpu.VMEM((tm, D_MODEL), BF16)],
        compiler_params=_cparams(("parallel", "arbitrary")),
        name="inproj",
    )(x2, g_mix, w_packed)


ATTN_ROWS = 32


def _attn_kernel(hp_ref, q_ref, kp_ref, kc_ref, kn_ref, vp_ref, vc_ref, vn_ref, o_ref, s_scr, p_scr, *, seq):
    j = pl.program_id(1)
    q = (q_ref[...] * (HEAD_DIM ** -0.5)).astype(BF16)
    kw = jnp.concatenate([kp_ref[...], kc_ref[...], kn_ref[...]], axis=0).astype(BF16)
    vw = jnp.concatenate([vp_ref[...], vc_ref[...], vn_ref[...]], axis=0).astype(BF16)
    qi = lax.broadcasted_iota(jnp.int32, (BLOCK, 3 * BLOCK), 0)
    km = lax.broadcasted_iota(jnp.int32, (BLOCK, 3 * BLOCK), 1)
    rel = qi - km + BLOCK
    key_pos = (j - 1) * BLOCK + km
    valid = (jnp.abs(rel) <= WINDOW) & (key_pos >= 0) & (key_pos < seq)
    dmask = jnp.where(valid, jnp.abs(rel).astype(F32), jnp.inf)

    for kv in range(N_KV_HEADS):
        qg = jnp.concatenate([q[:, (kv * Q_PER_KV + g) * HEAD_DIM:(kv * Q_PER_KV + g + 1) * HEAD_DIM]
                              for g in range(Q_PER_KV)], axis=0)
        kh = kw[:, kv * HEAD_DIM:(kv + 1) * HEAD_DIM]
        s_scr[kv] = lax.dot_general(qg, kh, (((1,), (1,)), ((), ())), preferred_element_type=F32)

    for h in range(N_Q_HEADS):
        kv, g = divmod(h, Q_PER_KV)
        sink = hp_ref[1, h]
        bias = hp_ref[0, h] * dmask
        for r0 in range(0, BLOCK, ATTN_ROWS):
            rows = slice(g * BLOCK + r0, g * BLOCK + r0 + ATTN_ROWS)
            s = s_scr[kv, rows, :] - bias[r0:r0 + ATTN_ROWS, :]
            mx = jnp.maximum(jnp.max(s, axis=-1, keepdims=True), sink)
            p = jnp.exp(s - mx)
            denom = jnp.sum(p, axis=-1, keepdims=True) + jnp.exp(sink - mx)
            p_scr[kv, rows, :] = (p / denom).astype(BF16)

    outs = []
    for kv in range(N_KV_HEADS):
        o = jnp.dot(p_scr[kv], vw[:, kv * HEAD_DIM:(kv + 1) * HEAD_DIM], preferred_element_type=F32)
        outs += [o[g * BLOCK:(g + 1) * BLOCK, :] for g in range(Q_PER_KV)]
    o_ref[...] = jnp.concatenate(outs, axis=-1).astype(o_ref.dtype)


def _attention(proj3, head_params):
    b, s, _ = proj3.shape
    nb = s // BLOCK
    kcol, vcol = P_K // KV_W, P_V // KV_W
    prev = lambda bb, j: jnp.maximum(j - 1, 0)
    nxt = lambda bb, j: jnp.minimum(j + 1, nb - 1)
    kv_spec = lambda col, f: pl.BlockSpec((None, BLOCK, KV_W), lambda bb, j: (bb, f(bb, j), col))
    cur = lambda bb, j: j
    return pl.pallas_call(
        functools.partial(_attn_kernel, seq=s),
        grid=(b, nb),
        in_specs=[
            pl.BlockSpec(memory_space=pltpu.SMEM),
            pl.BlockSpec((None, BLOCK, ATTN_W), lambda bb, j: (bb, j, P_Q // ATTN_W)),
            kv_spec(kcol, prev), kv_spec(kcol, cur), kv_spec(kcol, nxt),
            kv_spec(vcol, prev), kv_spec(vcol, cur), kv_spec(vcol, nxt),
        ],
        out_specs=pl.BlockSpec((None, BLOCK, ATTN_W), lambda bb, j: (bb, j, 0)),
        out_shape=jax.ShapeDtypeStruct((b, s, ATTN_W), BF16),
        scratch_shapes=[pltpu.VMEM((N_KV_HEADS, Q_PER_KV * BLOCK, 3 * BLOCK), F32),
                        pltpu.VMEM((N_KV_HEADS, Q_PER_KV * BLOCK, 3 * BLOCK), BF16)],
        compiler_params=_cparams(("parallel", "arbitrary")),
        name="attn",
    )(head_params, proj3, proj3, proj3, proj3, proj3, proj3, proj3)


HALO = 8
CONV_TB = 512
CONV_CB = 512


def _conv_kernel(cur_ref, prev_ref, next_ref, w_ref, b_ref, o_ref, ext):
    i = pl.program_id(1)
    last = pl.num_programs(1) - 1
    tb = cur_ref.shape[0]
    ext[0:HALO, :] = jnp.where(i == 0, 0.0, prev_ref[...])
    ext[HALO:HALO + tb, :] = cur_ref[...]
    ext[HALO + tb:, :] = jnp.where(i == last, 0.0, next_ref[...])
    half = D_CONV // 2
    acc = b_ref[...] + w_ref[0:1, :] * ext[HALO - half:HALO - half + tb, :]
    for k in range(1, D_CONV):
        acc = acc + w_ref[k:k + 1, :] * ext[HALO - half + k:HALO - half + k + tb, :]
    o_ref[...] = acc * _sigmoid(acc)


def _conv_silu(proj3, conv_w, conv_b):
    b, s, _ = proj3.shape
    tb, cb = CONV_TB, CONV_CB
    c0 = P_XBC // cb
    nrow8 = s // HALO
    return pl.pallas_call(
        _conv_kernel,
        grid=(b, s // tb, XBC_W // cb),
        in_specs=[
            pl.BlockSpec((None, tb, cb), lambda bb, i, c: (bb, i, c0 + c)),
            pl.BlockSpec((None, HALO, cb), lambda bb, i, c: (bb, jnp.maximum(i * (tb // HALO) - 1, 0), c0 + c)),
            pl.BlockSpec((None, HALO, cb),
                         lambda bb, i, c: (bb, jnp.minimum((i + 1) * (tb // HALO), nrow8 - 1), c0 + c)),
            pl.BlockSpec((D_CONV, cb), lambda bb, i, c: (0, c)),
            pl.BlockSpec((1, cb), lambda bb, i, c: (0, c)),
        ],
        out_specs=pl.BlockSpec((None, tb, cb), lambda bb, i, c: (bb, i, c)),
        out_shape=jax.ShapeDtypeStruct((b, s, XBC_W), F32),
        scratch_shapes=[pltpu.VMEM((tb + 2 * HALO, cb), F32)],
        compiler_params=_cparams(("parallel", "arbitrary", "arbitrary")),
        name="conv",
    )(proj3, proj3, proj3, conv_w, conv_b)


N_PAIRS = N_SSM_HEADS // 2
PAIRS_PER_GROUP = N_PAIRS // N_SSM_GROUPS
GROUP_W = D_INNER // N_SSM_GROUPS


def _softplus(x):
    return jnp.maximum(x, 0.0) + jnp.log1p(jnp.exp(-jnp.abs(x)))


def _ssd_kernel(*refs, reverse):
    if reverse:
        xbc_ref, dt_ref, alog_ref, dtb_ref, yf_ref, z_ref, dskip_ref, gn_ref, o_ref, state, ybuf = refs
    else:
        xbc_ref, dt_ref, alog_ref, dtb_ref, o_ref, state = refs
        ybuf = o_ref
    L = CHUNK

    @pl.when(pl.program_id(1) == 0)
    def _():
        state[...] = jnp.zeros_like(state)

    dt = _softplus(dt_ref[...] + dtb_ref[...])
    dta = dt * (-jnp.exp(alog_ref[...]))
    row = lax.broadcasted_iota(jnp.int32, (L, L), 0)
    col = lax.broadcasted_iota(jnp.int32, (L, L), 1)
    tri = (col >= row) if reverse else (col <= row)
    cum = jnp.dot(tri.astype(F32), dta, precision=lax.Precision.HIGHEST, preferred_element_type=F32)
    cum_t = cum.T
    dt_t = dt.T
    end = 0 if reverse else L - 1
    ecum = jnp.exp(cum)
    left = lax.broadcasted_iota(jnp.int32, (L, LANES), 1) < SSM_HEAD_DIM
    off = N_SSM_HEADS if reverse else 0

    for g in range(N_SSM_GROUPS):
        bg = xbc_ref[:, D_INNER + g * D_STATE:D_INNER + (g + 1) * D_STATE]
        cg = xbc_ref[:, D_INNER + (N_SSM_GROUPS + g) * D_STATE:D_INNER + (N_SSM_GROUPS + g + 1) * D_STATE]
        cgb = cg.astype(BF16)
        cb = lax.dot_general(cgb, bg.astype(BF16), (((1,), (1,)), ((), ())),
                             preferred_element_type=F32)
        bt = bg.T
        for pp in range(PAIRS_PER_GROUP):
            hp = g * PAIRS_PER_GROUP + pp
            h1, h2 = 2 * hp + off, 2 * hp + 1 + off
            xb = xbc_ref[:, hp * LANES:(hp + 1) * LANES].astype(BF16)
            zero = jnp.zeros_like(xb)
            xbd = jnp.concatenate([jnp.where(left, xb, zero), jnp.where(left, zero, xb)], axis=0)
            ms, bds, decs = [], [], []
            for h in (h1, h2):
                seg = cum[:, h:h + 1] - cum_t[h:h + 1, :]
                lmat = jnp.exp(jnp.where(tri, seg, NEG_BIG))
                ms.append((cb * lmat * dt_t[h:h + 1, :]).astype(BF16))
                tot = cum_t[h:h + 1, end:end + 1]
                bds.append((bt * (jnp.exp(tot - cum_t[h:h + 1, :]) * dt_t[h:h + 1, :])).astype(BF16))
                decs.append(jnp.exp(tot))
            sp = state[hp]
            y = jnp.dot(jnp.concatenate(ms, axis=1), xbd, preferred_element_type=F32)
            y_off = jnp.dot(cgb, sp.astype(BF16), preferred_element_type=F32)
            y = y + y_off * jnp.where(left, ecum[:, h1:h1 + 1], ecum[:, h2:h2 + 1])
            ybuf[:, hp * LANES:(hp + 1) * LANES] = y
            snew = jnp.dot(jnp.concatenate(bds, axis=1), xbd, preferred_element_type=F32)
            state[hp] = sp * jnp.where(left, decs[0], decs[1]) + snew

    if reverse:
        y = ybuf[...] + yf_ref[...] + dskip_ref[...] * xbc_ref[:, 0:D_INNER]
        z = z_ref[...]
        gated = y * (z * _sigmoid(z))
        for g in range(N_SSM_GROUPS):
            sl = slice(g * GROUP_W, (g + 1) * GROUP_W)
            o_ref[:, sl] = _rms(gated[:, sl], gn_ref[:, sl]).astype(o_ref.dtype)


def _ssd(xbc_act, proj3, alog_all, dtb_all, reverse, extras=()):
    b, s, _ = xbc_act.shape
    nc = s // CHUNK
    cidx = (lambda c: nc - 1 - c) if reverse else (lambda c: c)
    row_spec = lambda w, colblk: pl.BlockSpec((None, CHUNK, w), lambda bb, c: (bb, cidx(c), colblk))
    par_spec = lambda w: pl.BlockSpec((1, w), lambda bb, c: (0, 0))
    in_specs = [row_spec(XBC_W, 0), row_spec(LANES, P_DT // LANES), par_spec(LANES), par_spec(LANES)]
    args = [xbc_act, proj3, alog_all, dtb_all]
    scratch = [pltpu.VMEM((N_PAIRS, D_STATE, LANES), F32)]
    if reverse:
        yf, dskip, gnorm = extras
        in_specs += [row_spec(D_INNER, 0), row_spec(D_INNER, P_Z // D_INNER), par_spec(D_INNER), par_spec(D_INNER)]
        args += [yf, proj3, dskip, gnorm]
        scratch.append(pltpu.VMEM((CHUNK, D_INNER), F32))
        out_dtype = BF16
    else:
        out_dtype = F32
    return pl.pallas_call(
        functools.partial(_ssd_kernel, reverse=reverse),
        grid=(b, nc),
        in_specs=in_specs,
        out_specs=row_spec(D_INNER, 0),
        out_shape=jax.ShapeDtypeStruct((b, s, D_INNER), out_dtype),
        scratch_shapes=scratch,
        compiler_params=_cparams(("parallel", "arbitrary")),
        name="ssd_bwd" if reverse else "ssd_fwd",
    )(*args)


def _merge_kernel(attn_ref, ssm_ref, ga_ref, gs_ref, x_ref, wa_ref, ws_ref, wo_ref, gf_ref,
                  x1_ref, hn_ref, hnt_ref):
    a = jnp.dot(attn_ref[...], wa_ref[...], preferred_element_type=F32)
    s = jnp.dot(ssm_ref[...], ws_ref[...], preferred_element_type=F32)
    merged = _sigmoid(ga_ref[...]) * a + _sigmoid(gs_ref[...]) * s
    x1 = x_ref[...] + jnp.dot(merged.astype(BF16), wo_ref[...], preferred_element_type=F32)
    x1_ref[...] = x1
    hn = _rms(x1, gf_ref[...])
    hn_ref[...] = hn.astype(BF16)
    hnt_ref[...] = hn.T.astype(BF16)


def _resident(shape):
    return pl.BlockSpec(shape, lambda *_: (0,) * len(shape), pipeline_mode=pl.Buffered(1))


def _merge(attn2, ssm2, proj2, x2, wa, ws, wo, g_ffn, tm=256):
    t = x2.shape[0]
    rows = lambda w, colblk=0: pl.BlockSpec((tm, w), lambda i: (i, colblk))
    return pl.pallas_call(
        _merge_kernel,
        grid=(t // tm,),
        in_specs=[
            rows(ATTN_W), rows(D_INNER), rows(D_MODEL, P_GA // D_MODEL), rows(D_MODEL, P_GS // D_MODEL),
            rows(D_MODEL), _resident((ATTN_W, D_MODEL)), _resident((D_INNER, D_MODEL)),
            _resident((D_MODEL, D_MODEL)), _resident((1, D_MODEL)),
        ],
        out_specs=[rows(D_MODEL), rows(D_MODEL), pl.BlockSpec((D_MODEL, tm), lambda i: (0, i))],
        out_shape=[jax.ShapeDtypeStruct((t, D_MODEL), F32), jax.ShapeDtypeStruct((t, D_MODEL), BF16),
                   jax.ShapeDtypeStruct((D_MODEL, t), BF16)],
        compiler_params=_cparams(("parallel",)),
        name="merge",
    )(attn2, ssm2, proj2, proj2, x2, wa, ws, wo, g_ffn)


def _extract_topk(s, k_top):
    r = s.shape[0]
    iota = lax.broadcasted_iota(jnp.int32, s.shape, 0).astype(F32)
    pos = jnp.full(s.shape, float(k_top), F32)
    vals = []
    for k in range(k_top):
        m = jnp.max(s, axis=0, keepdims=True)
        idx = jnp.min(jnp.where(s == m, iota, float(r)), axis=0, keepdims=True)
        hit = iota == idx
        pos = jnp.where(hit, float(k), pos)
        s = jnp.where(hit, -jnp.inf, s)
        vals.append(m)
    return vals, pos


def _extract_topk_distinct(s, k_top):
    s0 = s
    vals = []
    for k in range(k_top):
        m = jnp.max(s, axis=0, keepdims=True)
        s = jnp.where(s == m, -jnp.inf, s)
        vals.append(m)
    removed = jnp.sum(jnp.where(s == -jnp.inf, 1.0, 0.0), axis=0, keepdims=True)
    pos = jnp.zeros(s.shape, F32)
    for k in range(k_top):
        pos = jnp.where(vals[k] > s0, float(k + 1), pos)
    return vals, pos, removed


def _any_lane(flags):
    acc = flags[0]
    for f in flags[1:]:
        acc = acc | f
    return jnp.max(jnp.where(acc, 1.0, 0.0)) > 0.0


def _stack_rows(rows, n_rows):
    n = rows[0].shape[1]
    iota = lax.broadcasted_iota(jnp.int32, (n_rows, n), 0)
    out = jnp.zeros((n_rows, n), F32)
    for k, row in enumerate(rows):
        out = jnp.where(iota == k, row, out)
    return out


CAND_PAIRS = [(k1, k2) for k1 in range(PEER_TOPK) for k2 in range(PEER_TOPK) if (k1 + 1) * (k2 + 1) <= PEER_TOPK]
CAND_ROWS = -(-len(CAND_PAIRS) // 8) * 8
ROUTE_ILP = 4


def _route_kernel(hn_ref, wq_ref, sk_ref, cnt_ref, a_ref, pos_ref, b_ref, sc_scr, pos_scr, val_scr):
    k_top = PEER_TOPK
    tm = hn_ref.shape[0]
    nch = tm // LANES
    q = jnp.dot(hn_ref[...], wq_ref[...], preferred_element_type=F32).astype(BF16)
    for v in range(2 * PEER_HEADS):
        qv = q[:, v * HALF_KEY:(v + 1) * HALF_KEY]
        s = lax.dot_general(sk_ref[v], qv, (((1,), (1,)), ((), ())), preferred_element_type=F32)
        for ch in range(nch):
            sc_scr[v, ch] = s[:, ch * LANES:(ch + 1) * LANES]

    def stage1(i, carry):
        v, grp = i // (nch // ROUTE_ILP), i % (nch // ROUTE_ILP)
        chunks = tuple(ROUTE_ILP * grp + d for d in range(ROUTE_ILP))
        flags = []
        for ch in chunks:
            vals, pos, removed = _extract_topk_distinct(sc_scr[v, ch], k_top)
            pos_scr[v, ch] = pos
            val_scr[v, ch] = _stack_rows(vals, k_top)
            flags.append(removed != float(k_top))

        @pl.when(_any_lane(flags))
        def _():
            for ch in chunks:
                vals, pos = _extract_topk(sc_scr[v, ch], k_top)
                pos_scr[v, ch] = pos
                val_scr[v, ch] = _stack_rows(vals, k_top)

        return carry

    lax.fori_loop(0, 2 * PEER_HEADS * (nch // ROUTE_ILP), stage1, 0)

    def candidates(h, ch):
        v1 = val_scr[2 * h, ch]
        v2 = val_scr[2 * h + 1, ch]
        iota_c = lax.broadcasted_iota(jnp.int32, (CAND_ROWS, LANES), 0)
        cand = jnp.full((CAND_ROWS, LANES), -jnp.inf, F32)
        for r, (k1, k2) in enumerate(CAND_PAIRS):
            cand = jnp.where(iota_c == r, v1[k1:k1 + 1, :] + v2[k2:k2 + 1, :], cand)
        return cand

    def emit(h, ch, best, cpos):
        cs = slice(ch * LANES, (ch + 1) * LANES)
        sel = jnp.where(cpos < float(k_top), 1.0, 0.0)
        zsum = jnp.zeros((1, LANES), F32)
        for k in range(k_top):
            zsum = zsum + jnp.exp(best[k] - best[0])
        cnt_rows = [jnp.zeros((1, LANES), F32) for _ in range(k_top)]
        for r, (k1, k2) in enumerate(CAND_PAIRS):
            cnt_rows[k1] = cnt_rows[k1] + sel[r:r + 1, :]
        pos1 = pos_scr[2 * h, ch]
        cnt = jnp.zeros((N_KEYS, LANES), F32)
        for k in range(k_top):
            cnt = jnp.where(pos1 == float(k), cnt_rows[k], cnt)
        cnt_ref[h, :, cs] = cnt
        a_ref[h, :, cs] = jnp.exp(sc_scr[2 * h, ch] - val_scr[2 * h, ch, 0:1, :])
        pos_ref[h, :, cs] = pos_scr[2 * h + 1, ch]
        b_ref[h, :, cs] = jnp.exp(sc_scr[2 * h + 1, ch] - val_scr[2 * h + 1, ch, 0:1, :]) / zsum

    def stage2(h, carry):
        flags = []
        for ch in range(nch):
            best, cpos, removed = _extract_topk_distinct(candidates(h, ch), k_top)
            emit(h, ch, best, cpos)
            flags.append(removed != float(k_top + CAND_ROWS - len(CAND_PAIRS)))

        @pl.when(_any_lane(flags))
        def _():
            for ch in range(nch):
                best, cpos = _extract_topk(candidates(h, ch), k_top)
                emit(h, ch, best, cpos)

        return carry

    lax.fori_loop(0, PEER_HEADS, stage2, 0)


def _route(hn2, wq, sk, tm=512):
    t = hn2.shape[0]
    nch = tm // LANES
    nv = 2 * PEER_HEADS
    out = jax.ShapeDtypeStruct((PEER_HEADS, N_KEYS, t), F32)
    ospec = pl.BlockSpec((PEER_HEADS, N_KEYS, tm), lambda i: (0, 0, i))
    return pl.pallas_call(
        _route_kernel,
        grid=(t // tm,),
        in_specs=[
            pl.BlockSpec((tm, D_MODEL), lambda i: (i, 0)),
            _resident((D_MODEL, nv * HALF_KEY)),
            _resident((nv, N_KEYS, HALF_KEY)),
        ],
        out_specs=[ospec] * 4,
        out_shape=[out] * 4,
        scratch_shapes=[pltpu.VMEM((nv, nch, N_KEYS, LANES), F32), pltpu.VMEM((nv, nch, N_KEYS, LANES), F32),
                        pltpu.VMEM((nv, nch, PEER_TOPK, LANES), F32)],
        compiler_params=_cparams(("parallel",)),
        name="route",
    )(hn2, wq, sk)


SQRT_HALF = math.sqrt(0.5)
SUBLANES = 8
PEER_TE = SUBLANES * N_KEYS
PEER_SUB = 64
PEER_MM_ROWS = 512
BF16_ROWS = 16


def _packed_rows(row):
    one = jnp.broadcast_to(row, (BF16_ROWS, LANES)).astype(BF16)
    return jnp.concatenate([one] * (PEER_SUB // BF16_ROWS), axis=0)


def _peer_kernel(hnt_ref, u_ref, vt_ref, cnt_ref, a_ref, pos_ref, b_ref, x1_ref, gf_ref, y_ref,
                 acc_ref, ht_ref, p_ref, pos_bf, b_bf):
    j = pl.program_id(1)
    te, tm = ht_ref.shape
    rows_per_tile = te // N_KEYS

    @pl.when(j == 0)
    def _():
        acc_ref[...] = jnp.zeros_like(acc_ref)
        pos_bf[...] = pos_ref[...].astype(BF16)
        b_bf[...] = b_ref[...].astype(BF16)

    for m0 in range(0, te, PEER_MM_ROWS):
        ms = slice(m0, m0 + PEER_MM_ROWS)
        ht_ref[ms, :] = jnp.dot(u_ref[ms, :], hnt_ref[...], preferred_element_type=F32)
    zero = jnp.zeros((PEER_SUB, LANES), BF16)
    for c in range(tm // LANES):
        cs = slice(c * LANES, (c + 1) * LANES)
        for r in range(rows_per_tile):
            cnt_rows = [_packed_rows(cnt_ref[h, j, r:r + 1, cs]) for h in range(PEER_HEADS)]
            a_rows = [_packed_rows(a_ref[h, j, r:r + 1, cs]) for h in range(PEER_HEADS)]
            for s0 in range(0, N_KEYS, PEER_SUB):
                w = None
                for h in range(PEER_HEADS):
                    hk = slice(h * N_KEYS + s0, h * N_KEYS + s0 + PEER_SUB)
                    term = jnp.where(pos_bf[hk, cs] < cnt_rows[h], b_bf[hk, cs] * a_rows[h], zero)
                    w = term if w is None else w + term
                rs = slice(r * N_KEYS + s0, r * N_KEYS + s0 + PEER_SUB)
                hr = ht_ref[rs, cs]
                act = 0.5 * hr * (1.0 + lax.erf(hr * SQRT_HALF))
                p_ref[rs, cs] = act.astype(BF16) * w
    for m0 in range(0, D_MODEL, PEER_MM_ROWS):
        ms = slice(m0, m0 + PEER_MM_ROWS)
        acc_ref[ms, :] += jnp.dot(vt_ref[ms, :], p_ref[...], preferred_element_type=F32)

    @pl.when(j == pl.num_programs(1) - 1)
    def _():
        y_ref[...] = _rms(x1_ref[...] + acc_ref[...].T, gf_ref[...])


def _peer(hnt, u_bf, vt_bf, route, x1, g_final, tm=512):
    t = hnt.shape[1]
    te = PEER_TE
    n_tiles = N_EXPERTS // te
    rows_per_tile = te // N_KEYS
    cnt, a, pos, b = route
    cnt = cnt.reshape(PEER_HEADS, n_tiles, rows_per_tile, t)
    a = a.reshape(PEER_HEADS, n_tiles, rows_per_tile, t)
    once = dict(pipeline_mode=pl.Buffered(1))
    pos = pos.reshape(PEER_HEADS * N_KEYS, t)
    b = b.reshape(PEER_HEADS * N_KEYS, t)
    rspec = pl.BlockSpec((PEER_HEADS * N_KEYS, tm), lambda i, j: (0, i), **once)
    tspec = pl.BlockSpec((PEER_HEADS, n_tiles, rows_per_tile, tm), lambda i, j: (0, 0, 0, i), **once)
    return pl.pallas_call(
        _peer_kernel,
        grid=(t // tm, n_tiles),
        in_specs=[
            pl.BlockSpec((D_MODEL, tm), lambda i, j: (0, i), **once),
            pl.BlockSpec((te, D_MODEL), lambda i, j: (j, 0)),
            pl.BlockSpec((None, D_MODEL, te), lambda i, j: (j, 0, 0)),
            tspec, tspec, rspec, rspec,
            pl.BlockSpec((tm, D_MODEL), lambda i, j: (i, 0), **once),
            pl.BlockSpec((1, D_MODEL), lambda i, j: (0, 0)),
        ],
        out_specs=pl.BlockSpec((tm, D_MODEL), lambda i, j: (i, 0)),
        out_shape=jax.ShapeDtypeStruct((t, D_MODEL), F32),
        scratch_shapes=[pltpu.VMEM((D_MODEL, tm), F32), pltpu.VMEM((te, tm), F32), pltpu.VMEM((te, tm), BF16),
                        pltpu.VMEM((PEER_HEADS * N_KEYS, tm), BF16), pltpu.VMEM((PEER_HEADS * N_KEYS, tm), BF16)],
        compiler_params=_cparams(("parallel", "arbitrary")),
        name="peer",
    )(hnt, u_bf, vt_bf, cnt, a, pos, b, x1, g_final)


def _pad_lanes(v):
    return jnp.pad(v, (0, LANES - v.shape[0])).reshape(1, LANES)


def _prep(g_mix, w_in, attn_sink, conv_w, conv_b, a_log_f, a_log_b, dt_bias_f, dt_bias_b, d_skip, g_ssm_norm,
          w_attn_o, w_ssm_o, w_out, g_ffn, w_query, sub_keys, expert_u, expert_v, g_final):
    w = w_in[0].astype(BF16)
    w_packed = jnp.concatenate(
        [w[:, _V_END:_Z_END], w[:, _DT_END:_IN_W], w[:, _Z_END:_XBC_END], w[:, :_V_END], w[:, _XBC_END:_DT_END],
         jnp.zeros((D_MODEL, LANES - 2 * N_SSM_HEADS), w.dtype)], axis=1)
    slopes = jnp.exp2(-8.0 * jnp.arange(1, N_Q_HEADS + 1, dtype=F32) / N_Q_HEADS)
    return dict(
        g_mix=g_mix.reshape(1, D_MODEL),
        w_packed=w_packed,
        head_params=jnp.stack([slopes, attn_sink[0].astype(F32)]),
        conv_w=conv_w[0], conv_b=conv_b.reshape(1, XBC_W),
        alog_all=_pad_lanes(jnp.concatenate([a_log_f[0], a_log_b[0]])),
        dtb_all=_pad_lanes(jnp.concatenate([dt_bias_f[0], dt_bias_b[0]])),
        dskip=jnp.repeat(d_skip[0], SSM_HEAD_DIM).reshape(1, D_INNER),
        gnorm=g_ssm_norm.reshape(1, D_INNER),
        wa=w_attn_o[0].astype(BF16), ws=w_ssm_o[0].astype(BF16), wo=w_out[0].astype(BF16),
        g_ffn=g_ffn.reshape(1, D_MODEL),
        wq=w_query[0].astype(BF16),
        sk=sub_keys[0].reshape(2 * PEER_HEADS, N_KEYS, HALF_KEY).astype(BF16),
        u_bf=expert_u[0].astype(BF16),
        vt_bf=expert_v[0].astype(BF16).reshape(N_EXPERTS // PEER_TE, PEER_TE, D_MODEL).transpose(0, 2, 1),
        g_final=g_final.reshape(1, D_MODEL),
    )


def _trunk(x, p):
    b, s, d = x.shape
    t = b * s
    x2 = x.reshape(t, d)
    proj2 = _inproj(x2, p["g_mix"], p["w_packed"])
    proj3 = proj2.reshape(b, s, P_W)
    attn = _attention(proj3, p["head_params"])
    xbc_act = _conv_silu(proj3, p["conv_w"], p["conv_b"])
    y_f = _ssd(xbc_act, proj3, p["alog_all"], p["dtb_all"], reverse=False)
    ssm = _ssd(xbc_act, proj3, p["alog_all"], p["dtb_all"], reverse=True, extras=(y_f, p["dskip"], p["gnorm"]))
    x1, hn, hnt = _merge(attn.reshape(t, ATTN_W), ssm.reshape(t, D_INNER), proj2, x2,
                    p["wa"], p["ws"], p["wo"], p["g_ffn"])
    route = _route(hn, p["wq"], p["sk"])
    y = _peer(hnt, p["u_bf"], p["vt_bf"], route, x1, p["g_final"])
    return y.reshape(b, s, d)


def kernel(x_prompt, x_sample, g_mix, w_in, attn_sink, conv_w, conv_b, a_log_f, a_log_b, dt_bias_f, dt_bias_b,
           d_skip, g_ssm_norm, w_attn_o, w_ssm_o, w_out, g_ffn, w_query, sub_keys, expert_u, expert_v, g_final):
    p = _prep(g_mix, w_in, attn_sink, conv_w, conv_b, a_log_f, a_log_b, dt_bias_f, dt_bias_b, d_skip, g_ssm_norm,
              w_attn_o, w_ssm_o, w_out, g_ffn, w_query, sub_keys, expert_u, expert_v, g_final)
    return (_trunk(x_prompt, p), _trunk(x_sample, p))
```
